```python
import math
import jax, jax.numpy as jnp
from jax import lax
import numpy as np

D_MODEL = 1024
BATCH = 32
SEQ = 2048
DEPTH = 4

GRID_W = 64
ROPE_THETA = 10000.0
Q_BLOCK = 128
EPS = 1e-6

MLA_HEADS = 16
MLA_NOPE = 64
MLA_ROPE = 32
MLA_QK = MLA_NOPE + MLA_ROPE
MLA_V = 64
Q_LORA = 384
KV_LORA = 256

GQA_HEADS = 16
GQA_KV_HEADS = 4
GQA_GROUP = GQA_HEADS // GQA_KV_HEADS
GQA_HD = D_MODEL // GQA_HEADS

D_FF = ((8 * D_MODEL + 3 * 256 - 1) // (3 * 256)) * 256

N_MIXERS = 2
N_MLA_LAYERS = (DEPTH + 1) // 2
N_GQA_LAYERS = DEPTH // 2

kernel_name = "interleaved_mla_gqa_axial_swiglu_encoder"


def rmsnorm(x, g):
    xf = x.astype(jnp.float32)
    y = xf * lax.rsqrt(jnp.mean(xf * xf, axis=-1, keepdims=True) + EPS)
    return (y * g.astype(jnp.float32)).astype(x.dtype)


def grid_positions(seq_len):
    rows = seq_len // GRID_W
    row = jnp.repeat(jnp.arange(rows, dtype=jnp.int32), GRID_W)
    col = jnp.tile(jnp.arange(GRID_W, dtype=jnp.int32), rows)
    return row, col


def rope_table(pos, dim):
    inv = ROPE_THETA ** (-jnp.arange(0, dim, 2, dtype=jnp.float32) / dim)
    ang = pos.astype(jnp.float32)[:, None] * inv[None, :]
    return jnp.cos(ang), jnp.sin(ang)


def axial_tables(row, col, rot_dim):
    half = rot_dim // 2
    cr, sr = rope_table(row, half)
    cc, sc = rope_table(col, half)
    return (cr, sr, cc, sc)


def rope_1d(x, cos, sin):
    d2 = x.shape[-1] // 2
    x1, x2 = x[..., :d2], x[..., d2:]
    c = cos[:, None, :]
    s = sin[:, None, :]
    return jnp.concatenate([x1 * c - x2 * s, x2 * c + x1 * s], axis=-1)


def axial_rope(x, tabs):
    cr, sr, cc, sc = tabs
    half = x.shape[-1] // 2
    xf = x.astype(jnp.float32)
    out = jnp.concatenate([rope_1d(xf[..., :half], cr, sr),
                           rope_1d(xf[..., half:], cc, sc)], axis=-1)
    return out.astype(x.dtype)


def blocked_attention(q, k, v, scale):
    B, S, HKV, G, Dk = q.shape
    nb = S // Q_BLOCK
    qb = q.reshape(B, nb, Q_BLOCK, HKV, G, Dk).transpose(1, 0, 2, 3, 4, 5)

    def one_block(q_blk):
        s = jnp.einsum('bqkgd,bskd->bkgqs', q_blk, k,
                       preferred_element_type=jnp.float32) * scale
        p = jax.nn.softmax(s, axis=-1).astype(v.dtype)
        return jnp.einsum('bkgqs,bskd->bqkgd', p, v)

    out = lax.map(one_block, qb)
    Dv = v.shape[-1]
    return out.transpose(1, 0, 2, 3, 4, 5).reshape(B, S, HKV * G, Dv)


def mla_mixer(h, w_in, q_lora_norm, w_uq, kv_lora_norm, w_ukv, q_norm, k_norm, w_o, tabs):
    B, S, _ = h.shape
    lat = h @ w_in
    c_q = rmsnorm(lat[..., :Q_LORA], q_lora_norm)
    c_kv = rmsnorm(lat[..., Q_LORA:Q_LORA + KV_LORA], kv_lora_norm)
    k_r = lat[..., Q_LORA + KV_LORA:][:, :, None, :]

    q = (c_q @ w_uq).reshape(B, S, MLA_HEADS, MLA_QK)
    kv = (c_kv @ w_ukv).reshape(B, S, MLA_HEADS, MLA_NOPE + MLA_V)
    k_nope, v = kv[..., :MLA_NOPE], kv[..., MLA_NOPE:]

    q_nope = rmsnorm(q[..., :MLA_NOPE], q_norm[:MLA_NOPE])
    q_rope = axial_rope(rmsnorm(q[..., MLA_NOPE:], q_norm[MLA_NOPE:]), tabs)
    k_nope = rmsnorm(k_nope, k_norm[:MLA_NOPE])
    k_rope = axial_rope(rmsnorm(k_r, k_norm[MLA_NOPE:]), tabs)
    k_rope = jnp.broadcast_to(k_rope, (B, S, MLA_HEADS, MLA_ROPE))

    q_full = jnp.concatenate([q_nope, q_rope], axis=-1)[:, :, :, None, :]
    k_full = jnp.concatenate([k_nope, k_rope], axis=-1)
    o = blocked_attention(q_full, k_full, v, MLA_QK ** -0.5)
    return o.reshape(B, S, MLA_HEADS * MLA_V) @ w_o


def gqa_mixer(h, w_qkv, q_norm, k_norm, w_o, tabs):
    B, S, _ = h.shape
    qkv = h @ w_qkv
    nq = GQA_HEADS * GQA_HD
    nk = GQA_KV_HEADS * GQA_HD
    q = qkv[..., :nq].reshape(B, S, GQA_HEADS, GQA_HD)
    k = qkv[..., nq:nq + nk].reshape(B, S, GQA_KV_HEADS, GQA_HD)
    v = qkv[..., nq + nk:].reshape(B, S, GQA_KV_HEADS, GQA_HD)
    q = axial_rope(rmsnorm(q, q_norm), tabs)
    k = axial_rope(rmsnorm(k, k_norm), tabs)
    q = q.reshape(B, S, GQA_KV_HEADS, GQA_GROUP, GQA_HD)
    o = blocked_attention(q, k, v, GQA_HD ** -0.5)
    return o.reshape(B, S, GQA_HEADS * GQA_HD) @ w_o


def swiglu(h, w_gate_up, w_down):
    gu = h @ w_gate_up
    g, u = gu[..., :D_FF], gu[..., D_FF:]
    return (jax.nn.silu(g) * u) @ w_down


def setup_inputs(seed: int = 0) -> dict:
    key = jax.random.key(seed)
    ks = jax.random.split(key, 20)

    def w(k, shape, fan_in):
        return jax.random.normal(k, shape, jnp.float32) * (fan_in ** -0.5)

    def gain(k, shape):
        return 1.0 + 0.02 * jax.random.normal(k, shape, jnp.float32)

    LA, LB, L = N_MLA_LAYERS, N_GQA_LAYERS, DEPTH
    return {
        "x": jax.random.normal(ks[0], (BATCH, SEQ, D_MODEL), jnp.float32),
        "mla_norm": gain(ks[1], (LA, D_MODEL)),
        "mla_w_in": w(ks[2], (LA, D_MODEL, Q_LORA + KV_LORA + MLA_ROPE), D_MODEL),
        "mla_q_lora_norm": gain(ks[3], (LA, Q_LORA)),
        "mla_w_uq": w(ks[4], (LA, Q_LORA, MLA_HEADS * MLA_QK), Q_LORA),
        "mla_kv_lora_norm": gain(ks[5], (LA, KV_LORA)),
        "mla_w_ukv": w(ks[6], (LA, KV_LORA, MLA_HEADS * (MLA_NOPE + MLA_V)), KV_LORA),
        "mla_q_norm": gain(ks[7], (LA, MLA_QK)),
        "mla_k_norm": gain(ks[8], (LA, MLA_QK)),
        "mla_w_o": w(ks[9], (LA, MLA_HEADS * MLA_V, D_MODEL), MLA_HEADS * MLA_V),
        "gqa_norm": gain(ks[10], (LB, D_MODEL)),
        "gqa_w_qkv": w(ks[11], (LB, D_MODEL, (GQA_HEADS + 2 * GQA_KV_HEADS) * GQA_HD), D_MODEL),
        "gqa_q_norm": gain(ks[12], (LB, GQA_HD)),
        "gqa_k_norm": gain(ks[13], (LB, GQA_HD)),
        "gqa_w_o": w(ks[14], (LB, GQA_HEADS * GQA_HD, D_MODEL), GQA_HEADS * GQA_HD),
        "ffn_norm": gain(ks[15], (L, D_MODEL)),
        "ffn_w_gate_up": w(ks[16], (L, D_MODEL, 2 * D_FF), D_MODEL),
        "ffn_w_down": w(ks[17], (L, D_FF, D_MODEL), D_FF),
    }


def reference(x, mla_norm, mla_w_in, mla_q_lora_norm, mla_w_uq, mla_kv_lora_norm,
              mla_w_ukv, mla_q_norm, mla_k_norm, mla_w_o, gqa_norm, gqa_w_qkv,
              gqa_q_norm, gqa_k_norm, gqa_w_o, ffn_norm, ffn_w_gate_up, ffn_w_down):
    S = x.shape[1]
    row, col = grid_positions(S)
    mla_tabs = axial_tables(row, col, MLA_ROPE)
    gqa_tabs = axial_tables(row, col, GQA_HD)

    for i in range(DEPTH):
        j = i // N_MIXERS
        if i % N_MIXERS == 0:
            h = rmsnorm(x, mla_norm[j])
            x = x + mla_mixer(h, mla_w_in[j], mla_q_lora_norm[j], mla_w_uq[j],
                              mla_kv_lora_norm[j], mla_w_ukv[j], mla_q_norm[j],
                              mla_k_norm[j], mla_w_o[j], mla_tabs)
        else:
            h = rmsnorm(x, gqa_norm[j])
            x = x + gqa_mixer(h, gqa_w_qkv[j], gqa_q_norm[j], gqa_k_norm[j],
                              gqa_w_o[j], gqa_tabs)
        h = rmsnorm(x, ffn_norm[i])
        x = x + swiglu(h, ffn_w_gate_up[i], ffn_w_down[i])
    return x
```

```python
import functools

import jax
import jax.numpy as jnp
from jax import lax
from jax.experimental import pallas as pl
from jax.experimental.pallas import tpu as pltpu

GRID_W = 64
ROPE_THETA = 10000.0
EPS = 1e-6

MLA_HEADS = 16
MLA_NOPE = 64
MLA_ROPE = 32
MLA_QK = MLA_NOPE + MLA_ROPE
MLA_V = 64
Q_LORA = 384
KV_LORA = 256

GQA_HEADS = 16
GQA_KV_HEADS = 4
GQA_HD = 64

LANES = 128
TOKEN_TILE = 512
Q_TILE = 512
FF_CHUNK = 256
MIB = 1024 * 1024

F32 = jnp.float32
BF16 = jnp.bfloat16

_NT = (((1,), (1,)), ((), ()))
_TN = (((0,), (0,)), ((), ()))


def _rms_tok(x, g):
    return x * lax.rsqrt(jnp.mean(x * x, axis=-1, keepdims=True) + EPS) * g


def _rms_feat(x, g):
    return x * lax.rsqrt(jnp.mean(x * x, axis=0, keepdims=True) + EPS) * g


def _head_rms(x3, g):
    return x3 * lax.rsqrt(jnp.mean(x3 * x3, axis=1, keepdims=True) + EPS) * g[None]


def _rope_feat(x3, tab_ref, nf):
    cr, sr, cc, sc = (tab_ref[i][None] for i in range(4))
    a, b, c, d = (x3[:, i * nf:(i + 1) * nf] for i in range(4))
    return jnp.concatenate([a * cr - b * sr, b * cr + a * sr, c * cc - d * sc, d * cc + c * sc], axis=1)


def _rope_tok(x, cos, sin_signed, nf):
    lane = lax.broadcasted_iota(jnp.int32, x.shape, 1)
    first = (lane % (2 * nf)) < nf
    partner = jnp.where(first, pltpu.roll(x, LANES - nf, 1), pltpu.roll(x, nf, 1))
    return x * cos + partner * sin_signed


def _head_block_rms(blk, real, g):
    ss = jnp.sum(blk * blk, axis=-1, keepdims=True)
    return blk * lax.rsqrt(ss * (1.0 / real) + EPS) * g


def _gqa_proj_kernel(x_ref, g_ref, wqT_ref, wk_ref, wvT_ref, gq_ref, gk_ref, tabq_ref, ck_ref, sk_ref,
                     qT_ref, k_ref, vT_ref):
    t = x_ref.shape[1]
    h = _rms_tok(x_ref[0], g_ref[...]).astype(BF16)
    qT = lax.dot_general(wqT_ref[...], h, _NT, preferred_element_type=F32)
    vT = lax.dot_general(wvT_ref[...], h, _NT, preferred_element_type=F32)
    kp = jnp.dot(h, wk_ref[...], preferred_element_type=F32)

    q3 = _head_rms(qT.reshape(GQA_HEADS, GQA_HD, t), gq_ref[...])
    qT_ref[0] = _rope_feat(q3, tabq_ref, GQA_HD // 4).astype(BF16)
    vT_ref[0] = vT.reshape(GQA_KV_HEADS, GQA_HD, t).astype(BF16)

    cos, sin_signed = ck_ref[...], sk_ref[...]
    for j in range(GQA_KV_HEADS):
        blk = _head_block_rms(kp[:, j * LANES:(j + 1) * LANES], GQA_HD, gk_ref[...])
        k_ref[0, :, j * LANES:(j + 1) * LANES] = _rope_tok(blk, cos, sin_signed, GQA_HD // 4).astype(BF16)


def _mla_proj_kernel(x_ref, g_ref, winT_ref, wink_ref, gqlT_ref, gkvlT_ref, gkvl_ref, wuqT_ref, wuvT_ref,
                     wuk_ref, gqn_ref, gkn_ref, gkr_ref, tabq_ref, ck_ref, sk_ref, qT_ref, k_ref, vT_ref):
    t = x_ref.shape[1]
    h = _rms_tok(x_ref[0], g_ref[...]).astype(BF16)

    latT = lax.dot_general(winT_ref[...], h, _NT, preferred_element_type=F32)
    cqT = _rms_feat(latT[:Q_LORA], gqlT_ref[...]).astype(BF16)
    ckvT = _rms_feat(latT[Q_LORA:], gkvlT_ref[...]).astype(BF16)

    qT = jnp.dot(wuqT_ref[...], cqT, preferred_element_type=F32)
    q3 = qT.reshape(MLA_HEADS, LANES, t)
    gqn = gqn_ref[...]
    nope = _head_rms(q3[:, :MLA_NOPE], gqn[:MLA_NOPE])
    rope = _head_rms(q3[:, MLA_NOPE:MLA_QK], gqn[MLA_NOPE:MLA_QK])
    rope = _rope_feat(rope, tabq_ref, MLA_ROPE // 4)
    pad = jnp.zeros((MLA_HEADS, LANES - MLA_QK, t), F32)
    qT_ref[0] = jnp.concatenate([nope, rope, pad], axis=1).astype(BF16)

    vT = jnp.dot(wuvT_ref[...], ckvT, preferred_element_type=F32)
    vT_ref[0] = vT.reshape(MLA_HEADS, MLA_V, t).astype(BF16)

    latk = jnp.dot(h, wink_ref[...], preferred_element_type=F32)
    ckv = _rms_tok(latk[:, :KV_LORA], gkvl_ref[...]).astype(BF16)
    kr = _head_block_rms(latk[:, KV_LORA:], MLA_ROPE, gkr_ref[...])
    kr = _rope_tok(kr, ck_ref[...], sk_ref[...], MLA_ROPE // 4)
    knp = jnp.dot(ckv, wuk_ref[...], preferred_element_type=F32)
    for j in range(MLA_HEADS):
        blk = _head_block_rms(knp[:, j * LANES:(j + 1) * LANES], MLA_NOPE, gkn_ref[...])
        k_ref[0, :, j * LANES:(j + 1) * LANES] = (blk + kr).astype(BF16)


def _attn_kernel(qT_ref, k_ref, vT_ref, oT_ref):
    dq, s = qT_ref.shape[2], qT_ref.shape[3]
    k = k_ref[0]
    vT = vT_ref[0, 0]
    for i in range(s // Q_TILE):
        sl = slice(i * Q_TILE, (i + 1) * Q_TILE)
        q = qT_ref[0, 0, :, sl]
        if dq < LANES:
            q = jnp.concatenate([q, jnp.zeros((LANES - dq, Q_TILE), q.dtype)], axis=0)
        sT = jnp.dot(k, q, preferred_element_type=F32)
        p = jnp.exp(sT - jnp.max(sT, axis=0, keepdims=True))
        inv = 1.0 / jnp.sum(p, axis=0, keepdims=True)
        oT = jnp.dot(vT, p.astype(BF16), preferred_element_type=F32)
        oT_ref[0, :, sl] = (oT * inv).astype(BF16)


def _out_ffn_kernel(x_ref, oT_ref, wo_ref, gf_ref, wg_ref, wu_ref, wd_ref, out_ref, act_ref):
    x1 = x_ref[0] + lax.dot_general(oT_ref[0], wo_ref[...], _TN, preferred_element_type=F32)
    h = _rms_tok(x1, gf_ref[...]).astype(BF16)
    for c in range(act_ref.shape[1] // FF_CHUNK):
        sl = slice(c * FF_CHUNK, (c + 1) * FF_CHUNK)
        g = jnp.dot(h, wg_ref[:, sl], preferred_element_type=F32)
        u = jnp.dot(h, wu_ref[:, sl], preferred_element_type=F32)
        act_ref[:, sl] = (g / (1.0 + jnp.exp(-g)) * u).astype(BF16)
    out_ref[0] = x1 + jnp.dot(act_ref[...], wd_ref[...], preferred_element_type=F32)


def _resident(shape):
    return pl.BlockSpec(shape, lambda *_: (0,) * len(shape), pipeline_mode=pl.Buffered(1))


def _params(vmem_mib, ndims):
    return pltpu.CompilerParams(dimension_semantics=("parallel",) * ndims, vmem_limit_bytes=vmem_mib * MIB)


def _rope_tables(s, rot_dim, lane_off):
    half = rot_dim // 2
    nf = half // 2
    tok = jnp.arange(s, dtype=jnp.int32)
    row = (tok // GRID_W).astype(F32)
    col = (tok % GRID_W).astype(F32)
    inv = ROPE_THETA ** (-jnp.arange(0, half, 2, dtype=F32) / half)
    ar = row[:, None] * inv[None, :]
    ac = col[:, None] * inv[None, :]
    cr, sr, cc, sc = jnp.cos(ar), jnp.sin(ar), jnp.cos(ac), jnp.sin(ac)
    tabq = jnp.stack([cr.T, sr.T, cc.T, sc.T])
    lo = jnp.zeros((s, lane_off), F32)
    hi = jnp.zeros((s, LANES - lane_off - rot_dim), F32)
    cos_k = jnp.concatenate([lo, cr, cr, cc, cc, hi], axis=1)
    sin_k = jnp.concatenate([lo, -sr, sr, -sc, sc, hi], axis=1)
    return tabq, cos_k, sin_k


def _pad_heads_cols(w, heads, d):
    w = w.reshape(w.shape[0], heads, d)
    return jnp.pad(w, ((0, 0), (0, 0), (0, LANES - d))).reshape(w.shape[0], heads * LANES)


def _gqa_proj(x, norm_g, w_qkv, q_norm, k_norm, tabs):
    b, s, d = x.shape
    t = TOKEN_TILE
    nq = GQA_HEADS * GQA_HD
    nk = GQA_KV_HEADS * GQA_HD
    tabq, cos_k, sin_k = tabs
    wqT = w_qkv[:, :nq].T.astype(BF16)
    wk = _pad_heads_cols(w_qkv[:, nq:nq + nk], GQA_KV_HEADS, GQA_HD).astype(BF16)
    wvT = w_qkv[:, nq + nk:].T.astype(BF16)
    gq = (q_norm * GQA_HD ** -0.5).reshape(GQA_HD, 1)
    gk = jnp.pad(k_norm, (0, LANES - GQA_HD)).reshape(1, LANES)
    nf = GQA_HD // 4
    return pl.pallas_call(
        _gqa_proj_kernel,
        grid=(b, s // t),
        in_specs=[
            pl.BlockSpec((1, t, d), lambda bi, i: (bi, i, 0)),
            _resident((1, d)),
            _resident((nq, d)),
            _resident((d, GQA_KV_HEADS * LANES)),
            _resident((nk, d)),
            _resident((GQA_HD, 1)),
            _resident((1, LANES)),
            pl.BlockSpec((4, nf, t), lambda bi, i: (0, 0, i)),
            pl.BlockSpec((t, LANES), lambda bi, i: (i, 0)),
            pl.BlockSpec((t, LANES), lambda bi, i: (i, 0)),
        ],
        out_specs=[
            pl.BlockSpec((1, GQA_HEADS, GQA_HD, t), lambda bi, i: (bi, 0, 0, i)),
            pl.BlockSpec((1, t, GQA_KV_HEADS * LANES), lambda bi, i: (bi, i, 0)),
            pl.BlockSpec((1, GQA_KV_HEADS, GQA_HD, t), lambda bi, i: (bi, 0, 0, i)),
        ],
        out_shape=[
            jax.ShapeDtypeStruct((b, GQA_HEADS, GQA_HD, s), BF16),
            jax.ShapeDtypeStruct((b, s, GQA_KV_HEADS * LANES), BF16),
            jax.ShapeDtypeStruct((b, GQA_KV_HEADS, GQA_HD, s), BF16),
        ],
        compiler_params=_params(40, 2),
        name="gqa_proj",
    )(x, norm_g.reshape(1, d), wqT, wk, wvT, gq, gk, tabq, cos_k, sin_k)


def _mla_proj(x, norm_g, w_in, q_lora_norm, w_uq, kv_lora_norm, w_ukv, q_norm, k_norm, tabs):
    b, s, d = x.shape
    t = TOKEN_TILE
    tabq, cos_k, sin_k = tabs
    nlat = Q_LORA + KV_LORA
    winT = w_in[:, :nlat].T.astype(BF16)
    w_kr = jnp.pad(w_in[:, nlat:], ((0, 0), (MLA_NOPE, LANES - MLA_QK)))
    wink = jnp.concatenate([w_in[:, Q_LORA:nlat], w_kr], axis=1).astype(BF16)
    wuq = w_uq.reshape(Q_LORA, MLA_HEADS, MLA_QK)
    wuqT = jnp.pad(wuq, ((0, 0), (0, 0), (0, LANES - MLA_QK))).reshape(Q_LORA, MLA_HEADS * LANES).T.astype(BF16)
    wukv = w_ukv.reshape(KV_LORA, MLA_HEADS, MLA_NOPE + MLA_V)
    wuk = _pad_heads_cols(wukv[:, :, :MLA_NOPE].reshape(KV_LORA, MLA_HEADS * MLA_NOPE), MLA_HEADS, MLA_NOPE).astype(BF16)
    wuvT = wukv[:, :, MLA_NOPE:].reshape(KV_LORA, MLA_HEADS * MLA_V).T.astype(BF16)
    gqn = jnp.pad(q_norm * MLA_QK ** -0.5, (0, LANES - MLA_QK)).reshape(LANES, 1)
    gkn = jnp.pad(k_norm[:MLA_NOPE], (0, LANES - MLA_NOPE)).reshape(1, LANES)
    gkr = jnp.pad(k_norm[MLA_NOPE:], (MLA_NOPE, LANES - MLA_QK)).reshape(1, LANES)
    nf = MLA_ROPE // 4
    return pl.pallas_call(
        _mla_proj_kernel,
        grid=(b, s // t),
        in_specs=[
            pl.BlockSpec((1, t, d), lambda bi, i: (bi, i, 0)),
            _resident((1, d)),
            _resident((nlat, d)),
            _resident((d, KV_LORA + LANES)),
            _resident((Q_LORA, 1)),
            _resident((KV_LORA, 1)),
            _resident((1, KV_LORA)),
            _resident((MLA_HEADS * LANES, Q_LORA)),
            _resident((MLA_HEADS * MLA_V, KV_LORA)),
            _resident((KV_LORA, MLA_HEADS * LANES)),
            _resident((LANES, 1)),
            _resident((1, LANES)),
            _resident((1, LANES)),
            pl.BlockSpec((4, nf, t), lambda bi, i: (0, 0, i)),
            pl.BlockSpec((t, LANES), lambda bi, i: (i, 0)),
            pl.BlockSpec((t, LANES), lambda bi, i: (i, 0)),
        ],
        out_specs=[
            pl.BlockSpec((1, MLA_HEADS, LANES, t), lambda bi, i: (bi, 0, 0, i)),
            pl.BlockSpec((1, t, MLA_HEADS * LANES), lambda bi, i: (bi, i, 0)),
            pl.BlockSpec((1, MLA_HEADS, MLA_V, t), lambda bi, i: (bi, 0, 0, i)),
        ],
        out_shape=[
            jax.ShapeDtypeStruct((b, MLA_HEADS, LANES, s), BF16),
            jax.ShapeDtypeStruct((b, s, MLA_HEADS * LANES), BF16),
            jax.ShapeDtypeStruct((b, MLA_HEADS, MLA_V, s), BF16),
        ],
        compiler_params=_params(48, 2),
        name="mla_proj",
    )(x, norm_g.reshape(1, d), winT, wink, q_lora_norm.reshape(Q_LORA, 1), kv_lora_norm.reshape(KV_LORA, 1),
      kv_lora_norm.reshape(1, KV_LORA), wuqT, wuvT, wuk, gqn, gkn, gkr, tabq, cos_k, sin_k)


def _attention(qT, k, vT, group):
    b, heads, dq, s = qT.shape
    dv = vT.shape[2]
    return pl.pallas_call(
        _attn_kernel,
        grid=(b, heads),
        in_specs=[
            pl.BlockSpec((1, 1, dq, s), lambda bi, hi: (bi, hi, 0, 0)),
            pl.BlockSpec((1, s, LANES), lambda bi, hi: (bi, 0, hi // group)),
            pl.BlockSpec((1, 1, dv, s), lambda bi, hi: (bi, hi // group, 0, 0)),
        ],
        out_specs=pl.BlockSpec((1, dv, s), lambda bi, hi: (bi, hi, 0)),
        out_shape=jax.ShapeDtypeStruct((b, heads * dv, s), BF16),
        compiler_params=_params(40, 2),
        name="attention",
    )(qT, k, vT)


def _out_ffn(x, oT, w_o, ffn_g, w_gate_up, w_down):
    b, s, d = x.shape
    t = TOKEN_TILE
    dff = w_down.shape[0]
    wg = w_gate_up[:, :dff].astype(BF16)
    wu = w_gate_up[:, dff:].astype(BF16)
    return pl.pallas_call(
        _out_ffn_kernel,
        grid=(b, s // t),
        in_specs=[
            pl.BlockSpec((1, t, d), lambda bi, i: (bi, i, 0)),
            pl.BlockSpec((1, oT.shape[1], t), lambda bi, i: (bi, 0, i)),
            _resident(w_o.shape),
            _resident((1, d)),
            _resident((d, dff)),
            _resident((d, dff)),
            _resident((dff, d)),
        ],
        out_specs=pl.BlockSpec((1, t, d), lambda bi, i: (bi, i, 0)),
        out_shape=jax.ShapeDtypeStruct((b, s, d), F32),
        scratch_shapes=[pltpu.VMEM((t, dff), BF16)],
        compiler_params=_params(56, 2),
        name="out_ffn",
    )(x, oT, w_o.astype(BF16), ffn_g.reshape(1, d), wg, wu, w_down.astype(BF16))


def kernel(x, mla_norm, mla_w_in, mla_q_lora_norm, mla_w_uq, mla_kv_lora_norm, mla_w_ukv, mla_q_norm, mla_k_norm, mla_w_o, gqa_norm, gqa_w_qkv, gqa_q_norm, gqa_k_norm, gqa_w_o, ffn_norm, ffn_w_gate_up, ffn_w_down):
    s = x.shape[1]
    mla_tabs = _rope_tables(s, MLA_ROPE, MLA_NOPE)
    gqa_tabs = _rope_tables(s, GQA_HD, 0)
    depth = ffn_norm.shape[0]
    for i in range(depth):
        j = i // 2
        if i % 2 == 0:
            qT, k, vT = _mla_proj(x, mla_norm[j], mla_w_in[j], mla_q_lora_norm[j], mla_w_uq[j],
                                  mla_kv_lora_norm[j], mla_w_ukv[j], mla_q_norm[j], mla_k_norm[j], mla_tabs)
            oT = _attention(qT, k, vT, 1)
            w_o = mla_w_o[j]
        else:
            qT, k, vT = _gqa_proj(x, gqa_norm[j], gqa_w_qkv[j], gqa_q_norm[j], gqa_k_norm[j], gqa_tabs)
            oT = _attention(qT, k, vT, GQA_HEADS // GQA_KV_HEADS)
            w_o = gqa_w_o[j]
        x = _out_ffn(x, oT, w_o, ffn_norm[i], ffn_w_gate_up[i], ffn_w_down[i])
    return x
```

```python
import math

import jax
import jax.numpy as jnp
from jax import lax
from jax.experimental import pallas as pl
from jax.experimental.pallas import tpu as pltpu

GRID_W = 64
ROPE_THETA = 10000.0
EPS = 1e-6

MLA_HEADS = 16
MLA_NOPE = 64
MLA_ROPE = 32
MLA_QK = MLA_NOPE + MLA_ROPE
MLA_V = 64
Q_LORA = 384
KV_LORA = 256

GQA_HEADS = 16
GQA_KV_HEADS = 4
GQA_HD = 64

LANES = 128
BF16_ROWS = 16
TOKEN_TILE = 512
Q_TILE = 512
HEAD_BLOCK = 8
FF_CHUNK = 256
MIB = 1024 * 1024
LOG2E = math.log2(math.e)

F32 = jnp.float32
BF16 = jnp.bfloat16

_NT = (((1,), (1,)), ((), ()))
_TN = (((0,), (0,)), ((), ()))


def _rms_tok(x, g):
    return x * lax.rsqrt(jnp.mean(x * x, axis=-1, keepdims=True) + EPS) * g


def _rms_feat(x, g):
    return x * lax.rsqrt(jnp.mean(x * x, axis=0, keepdims=True) + EPS) * g


def _head_rms(x3, g):
    return x3 * lax.rsqrt(jnp.mean(x3 * x3, axis=1, keepdims=True) + EPS) * g[None]


def _rope_feat(x3, tab_ref, nf):
    cr, sr, cc, sc = (tab_ref[i][None] for i in range(4))
    a, b, c, d = (x3[:, i * nf:(i + 1) * nf] for i in range(4))
    return jnp.concatenate([a * cr - b * sr, b * cr + a * sr, c * cc - d * sc, d * cc + c * sc], axis=1)


def _rope_tok(x, cos, sin_signed, nf):
    lane = lax.broadcasted_iota(jnp.int32, x.shape, 1)
    first = (lane % (2 * nf)) < nf
    partner = jnp.where(first, pltpu.roll(x, LANES - nf, 1), pltpu.roll(x, nf, 1))
    return x * cos + partner * sin_signed


def _head_block_rms(blk, real, g):
    ss = jnp.sum(blk * blk, axis=-1, keepdims=True)
    return blk * lax.rsqrt(ss * (1.0 / real) + EPS) * g


def _store_q_tiles(qT_ref, q3):
    for c in range(q3.shape[2] // Q_TILE):
        qT_ref[0, :, c] = q3[:, :, c * Q_TILE:(c + 1) * Q_TILE].astype(BF16)


def _store_v_with_ones(vT_ref, v3):
    heads, _, t = v3.shape
    row = lax.broadcasted_iota(jnp.int32, (heads, BF16_ROWS, t), 1)
    ones_row = jnp.where(row == 0, 1.0, 0.0).astype(F32)
    vT_ref[0] = jnp.concatenate([v3, ones_row], axis=1).astype(BF16)


def _gqa_proj_kernel(x_ref, g_ref, wqT_ref, wk_ref, wvT_ref, gq_ref, gk_ref, tabq_ref, ck_ref, sk_ref,
                     qT_ref, k_ref, vT_ref):
    t = x_ref.shape[1]
    h = _rms_tok(x_ref[0], g_ref[...]).astype(BF16)
    qT = lax.dot_general(wqT_ref[...], h, _NT, preferred_element_type=F32)
    vT = lax.dot_general(wvT_ref[...], h, _NT, preferred_element_type=F32)
    kp = jnp.dot(h, wk_ref[...], preferred_element_type=F32)

    q3 = _head_rms(qT.reshape(GQA_HEADS, GQA_HD, t), gq_ref[...])
    _store_q_tiles(qT_ref, _rope_feat(q3, tabq_ref, GQA_HD // 4))
    _store_v_with_ones(vT_ref, vT.reshape(GQA_KV_HEADS, GQA_HD, t))

    cos, sin_signed = ck_ref[...], sk_ref[...]
    for j in range(GQA_KV_HEADS):
        blk = _head_block_rms(kp[:, j * LANES:(j + 1) * LANES], GQA_HD, gk_ref[...])
        k_ref[0, j] = _rope_tok(blk, cos, sin_signed, GQA_HD // 4).astype(BF16)


def _mla_proj_kernel(x_ref, g_ref, winT_ref, wink_ref, gqlT_ref, gkvlT_ref, gkvl_ref, wuqT_ref, wuvT_ref,
                     wuk_ref, gqn_ref, gkn_ref, gkr_ref, tabq_ref, ck_ref, sk_ref, qT_ref, k_ref, vT_ref):
    t = x_ref.shape[1]
    h = _rms_tok(x_ref[0], g_ref[...]).astype(BF16)

    latT = lax.dot_general(winT_ref[...], h, _NT, preferred_element_type=F32)
    cqT = _rms_feat(latT[:Q_LORA], gqlT_ref[...]).astype(BF16)
    ckvT = _rms_feat(latT[Q_LORA:], gkvlT_ref[...]).astype(BF16)

    qT = jnp.dot(wuqT_ref[...], cqT, preferred_element_type=F32)
    q3 = qT.reshape(MLA_HEADS, LANES, t)
    gqn = gqn_ref[...]
    nope = _head_rms(q3[:, :MLA_NOPE], gqn[:MLA_NOPE])
    rope = _head_rms(q3[:, MLA_NOPE:MLA_QK], gqn[MLA_NOPE:MLA_QK])
    rope = _rope_feat(rope, tabq_ref, MLA_ROPE // 4)
    pad = jnp.zeros((MLA_HEADS, LANES - MLA_QK, t), F32)
    _store_q_tiles(qT_ref, jnp.concatenate([nope, rope, pad], axis=1))

    vT = jnp.dot(wuvT_ref[...], ckvT, preferred_element_type=F32)
    _store_v_with_ones(vT_ref, vT.reshape(MLA_HEADS, MLA_V, t))

    latk = jnp.dot(h, wink_ref[...], preferred_element_type=F32)
    ckv = _rms_tok(latk[:, :KV_LORA], gkvl_ref[...]).astype(BF16)
    kr = _head_block_rms(latk[:, KV_LORA:], MLA_ROPE, gkr_ref[...])
    kr = _rope_tok(kr, ck_ref[...], sk_ref[...], MLA_ROPE // 4)
    knp = jnp.dot(ckv, wuk_ref[...], preferred_element_type=F32)
    for j in range(MLA_HEADS):
        blk = _head_block_rms(knp[:, j * LANES:(j + 1) * LANES], MLA_NOPE, gkn_ref[...])
        k_ref[0, j] = (blk + kr).astype(BF16)


def _attn_kernel(qT_ref, k_ref, vT_ref, oT_ref, s_ref, p_ref):
    hb, nt, dq, tq = qT_ref.shape[1:]
    nk = k_ref.shape[1]
    dv = oT_ref.shape[2] // hb
    heads_per_kv = hb // nk
    n_items = hb * nt

    def scores(n):
        h, i = n // nt, n % nt
        q = qT_ref[0, h, i]
        if dq < LANES:
            q = jnp.concatenate([q, jnp.zeros((LANES - dq, tq), q.dtype)], axis=0)
        s = jnp.dot(k_ref[0, h // heads_per_kv], q, preferred_element_type=F32)
        s_ref[...] = s
        return jnp.max(s, axis=0, keepdims=True)

    def probs(m):
        p_ref[...] = jnp.exp2(s_ref[...] - m).astype(BF16)

    def output(n):
        h, i = n // nt, n % nt
        o = jnp.dot(vT_ref[0, h // heads_per_kv], p_ref[...], preferred_element_type=F32)
        out = (o[:dv] * (1.0 / o[dv:dv + 1])).astype(BF16)
        row0 = h * dv if isinstance(h, int) else pl.multiple_of(h * dv, dv)
        oT_ref[0, i, pl.ds(row0, dv), :] = out

    m = scores(0)
    probs(m)
    m = scores(1)

    def body(n, m):
        output(n - 1)
        probs(m)
        return scores(n + 1)

    m = lax.fori_loop(1, n_items - 1, body, m)
    output(n_items - 2)
    probs(m)
    output(n_items - 1)


def _out_ffn_kernel(x_ref, oT_ref, wo_ref, gf_ref, wg_ref, wu_ref, wd_ref, out_ref, act_ref):
    y = [lax.dot_general(oT_ref[0, c], wo_ref[...], _TN, preferred_element_type=F32) for c in range(oT_ref.shape[1])]
    x1 = x_ref[0] + (y[0] if len(y) == 1 else jnp.concatenate(y, axis=0))
    h =_rms_tok(x1, gf_ref[...]).astype(BF16)
    for c in range(act_ref.shape[1] // FF_CHUNK):
        sl = slice(c * FF_CHUNK, (c + 1) * FF_CHUNK)
        g = jnp.dot(h, wg_ref[:, sl], preferred_element_type=F32)
        u = jnp.dot(h, wu_ref[:, sl], preferred_element_type=F32)
        act_ref[:, sl] = (g / (1.0 + jnp.exp(-g)) * u).astype(BF16)
    out_ref[0] = x1 + jnp.dot(act_ref[...], wd_ref[...], preferred_element_type=F32)


def _resident(shape):
    return pl.BlockSpec(shape, lambda *_: (0,) * len(shape), pipeline_mode=pl.Buffered(1))


def _params(vmem_mib, ndims):
    return pltpu.CompilerParams(dimension_semantics=("parallel",) * ndims, vmem_limit_bytes=vmem_mib * MIB)


def _rope_tables(s, rot_dim, lane_off):
    half = rot_dim // 2
    tok = jnp.arange(s, dtype=jnp.int32)
    row = (tok // GRID_W).astype(F32)
    col = (tok % GRID_W).astype(F32)
    inv = ROPE_THETA ** (-jnp.arange(0, half, 2, dtype=F32) / half)
    ar = row[:, None] * inv[None, :]
    ac = col[:, None] * inv[None, :]
    cr, sr, cc, sc = jnp.cos(ar), jnp.sin(ar), jnp.cos(ac), jnp.sin(ac)
    tabq = jnp.stack([cr.T, sr.T, cc.T, sc.T])
    lo = jnp.zeros((s, lane_off), F32)
    hi = jnp.zeros((s, LANES - lane_off - rot_dim), F32)
    cos_k = jnp.concatenate([lo, cr, cr, cc, cc, hi], axis=1)
    sin_k = jnp.concatenate([lo, -sr, sr, -sc, sc, hi], axis=1)
    return tabq, cos_k, sin_k


def _pad_heads_cols(w, heads, d):
    w = w.reshape(w.shape[0], heads, d)
    return jnp.pad(w, ((0, 0), (0, 0), (0, LANES - d))).reshape(w.shape[0], heads * LANES)


def _proj_out(b, s, t, heads, kv_heads, dq, dv):
    specs = [
        pl.BlockSpec((1, heads, t // Q_TILE, dq, Q_TILE), lambda bi, i: (bi, 0, i, 0, 0)),
        pl.BlockSpec((1, kv_heads, t, LANES), lambda bi, i: (bi, 0, i, 0)),
        pl.BlockSpec((1, kv_heads, dv + BF16_ROWS, t), lambda bi, i: (bi, 0, 0, i)),
    ]
    shapes = [
        jax.ShapeDtypeStruct((b, heads, s // Q_TILE, dq, Q_TILE), BF16),
        jax.ShapeDtypeStruct((b, kv_heads, s, LANES), BF16),
        jax.ShapeDtypeStruct((b, kv_heads, dv + BF16_ROWS, s), BF16),
    ]
    return specs, shapes


def _gqa_proj(x, norm_g, w_qkv, q_norm, k_norm, tabs):
    b, s, d = x.shape
    t = TOKEN_TILE
    nq = GQA_HEADS * GQA_HD
    nk = GQA_KV_HEADS * GQA_HD
    tabq, cos_k, sin_k = tabs
    wqT = w_qkv[:, :nq].T.astype(BF16)
    wk = _pad_heads_cols(w_qkv[:, nq:nq + nk], GQA_KV_HEADS, GQA_HD).astype(BF16)
    wvT = w_qkv[:, nq + nk:].T.astype(BF16)
    gq = (q_norm * (GQA_HD ** -0.5 * LOG2E)).reshape(GQA_HD, 1)
    gk = jnp.pad(k_norm, (0, LANES - GQA_HD)).reshape(1, LANES)
    nf = GQA_HD // 4
    out_specs, out_shape = _proj_out(b, s, t, GQA_HEADS, GQA_KV_HEADS, GQA_HD, GQA_HD)
    return pl.pallas_call(
        _gqa_proj_kernel,
        grid=(b, s // t),
        in_specs=[
            pl.BlockSpec((1, t, d), lambda bi, i: (bi, i, 0)),
            _resident((1, d)),
            _resident((nq, d)),
            _resident((d, GQA_KV_HEADS * LANES)),
            _resident((nk, d)),
            _resident((GQA_HD, 1)),
            _resident((1, LANES)),
            pl.BlockSpec((4, nf, t), lambda bi, i: (0, 0, i)),
            pl.BlockSpec((t, LANES), lambda bi, i: (i, 0)),
            pl.BlockSpec((t, LANES), lambda bi, i: (i, 0)),
        ],
        out_specs=out_specs,
        out_shape=out_shape,
        compiler_params=_params(40, 2),
        name="gqa_proj",
    )(x, norm_g.reshape(1, d), wqT, wk, wvT, gq, gk, tabq, cos_k, sin_k)


def _mla_proj(x, norm_g, w_in, q_lora_norm, w_uq, kv_lora_norm, w_ukv, q_norm, k_norm, tabs):
    b, s, d = x.shape
    t = TOKEN_TILE
    tabq, cos_k, sin_k = tabs
    nlat = Q_LORA + KV_LORA
    winT = w_in[:, :nlat].T.astype(BF16)
    w_kr = jnp.pad(w_in[:, nlat:], ((0, 0), (MLA_NOPE, LANES - MLA_QK)))
    wink = jnp.concatenate([w_in[:, Q_LORA:nlat], w_kr], axis=1).astype(BF16)
    wuq = w_uq.reshape(Q_LORA, MLA_HEADS, MLA_QK)
    wuqT = jnp.pad(wuq, ((0, 0), (0, 0), (0, LANES - MLA_QK))).reshape(Q_LORA, MLA_HEADS * LANES).T.astype(BF16)
    wukv = w_ukv.reshape(KV_LORA, MLA_HEADS, MLA_NOPE + MLA_V)
    wuk = _pad_heads_cols(wukv[:, :, :MLA_NOPE].reshape(KV_LORA, MLA_HEADS * MLA_NOPE), MLA_HEADS, MLA_NOPE).astype(BF16)
    wuvT = wukv[:, :, MLA_NOPE:].reshape(KV_LORA, MLA_HEADS * MLA_V).T.astype(BF16)
    gqn = jnp.pad(q_norm * (MLA_QK ** -0.5 * LOG2E), (0, LANES - MLA_QK)).reshape(LANES, 1)
    gkn = jnp.pad(k_norm[:MLA_NOPE], (0, LANES - MLA_NOPE)).reshape(1, LANES)
    gkr = jnp.pad(k_norm[MLA_NOPE:], (MLA_NOPE, LANES - MLA_QK)).reshape(1, LANES)
    nf = MLA_ROPE // 4
    out_specs, out_shape = _proj_out(b, s, t, MLA_HEADS, MLA_HEADS, LANES, MLA_V)
    return pl.pallas_call(
        _mla_proj_kernel,
        grid=(b, s // t),
        in_specs=[
            pl.BlockSpec((1, t, d), lambda bi, i: (bi, i, 0)),
            _resident((1, d)),
            _resident((nlat, d)),
            _resident((d, KV_LORA + LANES)),
            _resident((Q_LORA, 1)),
            _resident((KV_LORA, 1)),
            _resident((1, KV_LORA)),
            _resident((MLA_HEADS * LANES, Q_LORA)),
            _resident((MLA_HEADS * MLA_V, KV_LORA)),
            _resident((KV_LORA, MLA_HEADS * LANES)),
            _resident((LANES, 1)),
            _resident((1, LANES)),
            _resident((1, LANES)),
            pl.BlockSpec((4, nf, t), lambda bi, i: (0, 0, i)),
            pl.BlockSpec((t, LANES), lambda bi, i: (i, 0)),
            pl.BlockSpec((t, LANES), lambda bi, i: (i, 0)),
        ],
        out_specs=out_specs,
        out_shape=out_shape,
        compiler_params=_params(48, 2),
        name="mla_proj",
    )(x, norm_g.reshape(1, d), winT, wink, q_lora_norm.reshape(Q_LORA, 1), kv_lora_norm.reshape(KV_LORA, 1),
      kv_lora_norm.reshape(1, KV_LORA), wuqT, wuvT, wuk, gqn, gkn, gkr, tabq, cos_k, sin_k)


def _attention(qT, k, vT):
    b, heads, nt, dq, tq = qT.shape
    kv_heads, s = k.shape[1], k.shape[2]
    dva = vT.shape[2]
    dv = dva - BF16_ROWS
    hb = HEAD_BLOCK
    nk = hb * kv_heads // heads
    kv_map = lambda bi, g: (bi, g, 0, 0)
    return pl.pallas_call(
        _attn_kernel,
        grid=(b, heads // hb),
        in_specs=[
            pl.BlockSpec((1, hb, nt, dq, tq), lambda bi, g: (bi, g, 0, 0, 0)),
            pl.BlockSpec((1, nk, s, LANES), kv_map),
            pl.BlockSpec((1, nk, dva, s), kv_map),
        ],
        out_specs=pl.BlockSpec((1, nt, hb * dv, tq), lambda bi, g: (bi, 0, g, 0)),
        out_shape=jax.ShapeDtypeStruct((b, nt, heads * dv, tq), BF16),
        scratch_shapes=[pltpu.VMEM((s, tq), F32), pltpu.VMEM((s, tq), BF16)],
        compiler_params=_params(48, 2),
        name="attention",
    )(qT, k, vT)


def _out_ffn(x, oT, w_o, ffn_g, w_gate_up, w_down):
    b, s, d = x.shape
    t = TOKEN_TILE
    dff = w_down.shape[0]
    tq = oT.shape[3]
    wg = w_gate_up[:, :dff].astype(BF16)
    wu = w_gate_up[:, dff:].astype(BF16)
    return pl.pallas_call(
        _out_ffn_kernel,
        grid=(b, s // t),
        in_specs=[
            pl.BlockSpec((1, t, d), lambda bi, i: (bi, i, 0)),
            pl.BlockSpec((1, t // tq, oT.shape[2], tq), lambda bi, i: (bi, i, 0, 0)),
            _resident(w_o.shape),
            _resident((1, d)),
            _resident((d, dff)),
            _resident((d, dff)),
            _resident((dff, d)),
        ],
        out_specs=pl.BlockSpec((1, t, d), lambda bi, i: (bi, i, 0)),
        out_shape=jax.ShapeDtypeStruct((b, s, d), F32),
        scratch_shapes=[pltpu.VMEM((t, dff), BF16)],
        compiler_params=_params(56, 2),
        name="out_ffn",
    )(x, oT, w_o.astype(BF16), ffn_g.reshape(1, d), wg, wu, w_down.astype(BF16))


def kernel(x, mla_norm, mla_w_in, mla_q_lora_norm, mla_w_uq, mla_kv_lora_norm, mla_w_ukv, mla_q_norm, mla_k_norm, mla_w_o, gqa_norm, gqa_w_qkv, gqa_q_norm, gqa_k_norm, gqa_w_o, ffn_norm, ffn_w_gate_up, ffn_w_down):
    s = x.shape[1]
    mla_tabs = _rope_tables(s, MLA_ROPE, MLA_NOPE)
    gqa_tabs = _rope_tables(s, GQA_HD, 0)
    depth = ffn_norm.shape[0]
    for i in range(depth):
        j = i // 2
        if i % 2 == 0:
            qT, k, vT = _mla_proj(x, mla_norm[j], mla_w_in[j], mla_q_lora_norm[j], mla_w_uq[j],
                                  mla_kv_lora_norm[j], mla_w_ukv[j], mla_q_norm[j], mla_k_norm[j], mla_tabs)
            w_o = mla_w_o[j]
        else:
            qT, k, vT = _gqa_proj(x, gqa_norm[j], gqa_w_qkv[j], gqa_q_norm[j], gqa_k_norm[j], gqa_tabs)
            w_o = gqa_w_o[j]
        x = _out_ffn(x, _attention(qT, k, vT), w_o, ffn_norm[i], ffn_w_gate_up[i], ffn_w_down[i])
    return x
```

```python
import functools
import math

import jax
import jax.numpy as jnp
from jax import lax
from jax.experimental import pallas as pl
from jax.experimental.pallas import tpu as pltpu

GRID_W = 64
ROPE_THETA = 10000.0
EPS = 1e-6

MLA_HEADS = 16
MLA_NOPE = 64
MLA_ROPE = 32
MLA_QK = MLA_NOPE + MLA_ROPE
MLA_V = 64
Q_LORA = 384
KV_LORA = 256

GQA_HEADS = 16
GQA_KV_HEADS = 4
GQA_HD = 64

LANES = 128
BF16_ROWS = 16
TOKEN_TILE = 512
GQA_SUB_TILE = 256
MLA_SUB_TILE = 512
Q_TILE = 512
FF_CHUNK = 256
MIB = 1024 * 1024
ATTN_BLOCK_BUDGET = 36 * MIB
LOG2E = math.log2(math.e)

F32 = jnp.float32
BF16 = jnp.bfloat16

_NT = (((1,), (1,)), ((), ()))
_TN = (((0,), (0,)), ((), ()))


def _rms_tok(x, g):
    return x * lax.rsqrt(jnp.mean(x * x, axis=-1, keepdims=True) + EPS) * g


def _rms_feat(x, g):
    return x * lax.rsqrt(jnp.mean(x * x, axis=0, keepdims=True) + EPS) * g


def _rope_feat(x, tab, nf):
    cr, sr, cc, sc = tab[0], tab[1], tab[2], tab[3]
    a, b, c, d = (x[i * nf:(i + 1) * nf] for i in range(4))
    return jnp.concatenate([a * cr - b * sr, b * cr + a * sr, c * cc - d * sc, d * cc + c * sc], axis=0)


def _rope_tok(x, cos, sin_signed, nf):
    lane = lax.broadcasted_iota(jnp.int32, x.shape, 1)
    first = (lane % (2 * nf)) < nf
    partner = jnp.where(first, pltpu.roll(x, LANES - nf, 1), pltpu.roll(x, nf, 1))
    return x * cos + partner * sin_signed


def _head_block_rms(blk, real, g):
    ss = jnp.sum(blk * blk, axis=-1, keepdims=True)
    return blk * lax.rsqrt(ss * (1.0 / real) + EPS) * g


def _store_q(qT_ref, qT, heads, tok):
    q3 = qT.reshape(heads, qT.shape[0] // heads, qT.shape[1]).astype(BF16)
    c, off = tok.start // Q_TILE, tok.start % Q_TILE
    qT_ref[0, :, c, :, off:off + q3.shape[2]] = q3


def _store_v_with_ones(vT_ref, v3, tok):
    heads, _, n = v3.shape
    row = lax.broadcasted_iota(jnp.int32, (heads, BF16_ROWS, n), 1)
    ones_row = jnp.where(row == 0, 1.0, 0.0).astype(F32)
    vT_ref[0, :, :, tok] = jnp.concatenate([v3, ones_row], axis=1).astype(BF16)


def _sub_tiles(t, sub):
    return [slice(c * sub, (c + 1) * sub) for c in range(t // sub)]


def _gqa_proj_kernel(x_ref, g_ref, wqT_ref, wk_ref, wvT_ref, gk_ref, ck_ref, sk_ref, qT_ref, k_ref, vT_ref):
    for tok in _sub_tiles(x_ref.shape[1], GQA_SUB_TILE):
        n = tok.stop - tok.start
        h = _rms_tok(x_ref[0, tok], g_ref[...]).astype(BF16)
        qT = lax.dot_general(wqT_ref[...], h, _NT, preferred_element_type=F32)
        vT = lax.dot_general(wvT_ref[...], h, _NT, preferred_element_type=F32)
        kp = jnp.dot(h, wk_ref[...], preferred_element_type=F32)

        _store_q(qT_ref, qT, GQA_HEADS, tok)
        _store_v_with_ones(vT_ref, vT.reshape(GQA_KV_HEADS, GQA_HD, n), tok)

        cos, sin_signed = ck_ref[tok], sk_ref[tok]
        for j in range(GQA_KV_HEADS):
            blk = _head_block_rms(kp[:, j * LANES:(j + 1) * LANES], GQA_HD, gk_ref[...])
            k_ref[0, j, tok] = _rope_tok(blk, cos, sin_signed, GQA_HD // 4).astype(BF16)


def _mla_proj_kernel(x_ref, g_ref, winT_ref, wink_ref, gqlT_ref, gkvlT_ref, gkvl_ref, wuqT_ref, wuvT_ref,
                     wuk_ref, gkn_ref, gkr_ref, ck_ref, sk_ref, qT_ref, k_ref, vT_ref):
    for tok in _sub_tiles(x_ref.shape[1], MLA_SUB_TILE):
        n = tok.stop - tok.start
        h = _rms_tok(x_ref[0, tok], g_ref[...]).astype(BF16)

        latT = lax.dot_general(winT_ref[...], h, _NT, preferred_element_type=F32)
        cqT = _rms_feat(latT[:Q_LORA], gqlT_ref[...]).astype(BF16)
        ckvT = _rms_feat(latT[Q_LORA:], gkvlT_ref[...]).astype(BF16)

        _store_q(qT_ref, jnp.dot(wuqT_ref[...], cqT, preferred_element_type=F32), MLA_HEADS, tok)
        vT = jnp.dot(wuvT_ref[...], ckvT, preferred_element_type=F32)
        _store_v_with_ones(vT_ref, vT.reshape(MLA_HEADS, MLA_V, n), tok)

        latk = jnp.dot(h, wink_ref[...], preferred_element_type=F32)
        ckv = _rms_tok(latk[:, :KV_LORA], gkvl_ref[...]).astype(BF16)
        kr = _head_block_rms(latk[:, KV_LORA:], MLA_ROPE, gkr_ref[...])
        kr = _rope_tok(kr, ck_ref[tok], sk_ref[tok], MLA_ROPE // 4)
        knp = jnp.dot(ckv, wuk_ref[...], preferred_element_type=F32)
        for j in range(MLA_HEADS):
            blk = _head_block_rms(knp[:, j * LANES:(j + 1) * LANES], MLA_NOPE, gkn_ref[...])
            k_ref[0, j, tok] = (blk + kr).astype(BF16)


def _attn_kernel(qT_ref, k_ref, vT_ref, gq_ref, tab_ref, oT_ref, qn_ref, s_ref, p_ref, *, nope):
    hb, nt, dq, tq = qT_ref.shape[1:]
    nk = k_ref.shape[1]
    nf = tab_ref.shape[2]
    dv = oT_ref.shape[2] // hb
    heads_per_kv = hb // nk
    n_items = hb * nt

    def prep(n):
        h, i = n // nt, n % nt
        q = qT_ref[0, h, i].astype(F32)
        g = gq_ref[...]
        parts = [_rms_feat(q[:nope], g[:nope])] if nope else []
        parts.append(_rope_feat(_rms_feat(q[nope:], g[nope:]), tab_ref[i], nf))
        parts.append(jnp.zeros((LANES - dq, tq), F32))
        qn_ref[...] = jnp.concatenate(parts, axis=0).astype(BF16)

    def scores(n):
        s = jnp.dot(k_ref[0, (n // nt) // heads_per_kv], qn_ref[...], preferred_element_type=F32)
        s_ref[...] = s
        return jnp.max(s, axis=0, keepdims=True)

    def probs(m):
        p_ref[...] = jnp.exp2(s_ref[...] - m).astype(BF16)

    def output(n):
        h, i = n // nt, n % nt
        o = jnp.dot(vT_ref[0, h // heads_per_kv], p_ref[...], preferred_element_type=F32)
        out = (o[:dv] * (1.0 / o[dv:dv + 1])).astype(BF16)
        row0 = h * dv if isinstance(h, int) else pl.multiple_of(h * dv, dv)
        oT_ref[0, i, pl.ds(row0, dv), :] = out

    prep(0)
    m = scores(0)
    prep(1)
    probs(m)
    m = scores(1)
    prep(2)

    def body(n, m):
        output(n - 1)
        probs(m)
        m = scores(n + 1)
        prep(jnp.minimum(n + 2, n_items - 1))
        return m

    m = lax.fori_loop(1, n_items - 1, body, m)
    output(n_items - 2)
    probs(m)
    output(n_items - 1)


def _out_ffn_kernel(x_ref, oT_ref, wo_ref, gf_ref, wg_ref, wu_ref, wd_ref, out_ref, act_ref):
    y = [lax.dot_general(oT_ref[0, c], wo_ref[...], _TN, preferred_element_type=F32) for c in range(oT_ref.shape[1])]
    x1 = x_ref[0] + (y[0] if len(y) == 1 else jnp.concatenate(y, axis=0))
    h = _rms_tok(x1, gf_ref[...]).astype(BF16)
    for c in range(act_ref.shape[1] // FF_CHUNK):
        sl = slice(c * FF_CHUNK, (c + 1) * FF_CHUNK)
        g = jnp.dot(h, wg_ref[:, sl], preferred_element_type=F32)
        u = jnp.dot(h, wu_ref[:, sl], preferred_element_type=F32)
        act_ref[:, sl] = (g / (1.0 + jnp.exp(-g)) * u).astype(BF16)
    out_ref[0] = x1 + jnp.dot(act_ref[...], wd_ref[...], preferred_element_type=F32)


def _resident(shape):
    return pl.BlockSpec(shape, lambda *_: (0,) * len(shape), pipeline_mode=pl.Buffered(1))


def _params(vmem_mib, ndims):
    return pltpu.CompilerParams(dimension_semantics=("parallel",) * ndims, vmem_limit_bytes=vmem_mib * MIB)


def _rope_tables(s, rot_dim, lane_off):
    half = rot_dim // 2
    nf = half // 2
    tok = jnp.arange(s, dtype=jnp.int32)
    row = (tok // GRID_W).astype(F32)
    col = (tok % GRID_W).astype(F32)
    inv = ROPE_THETA ** (-jnp.arange(0, half, 2, dtype=F32) / half)
    ar = row[:, None] * inv[None, :]
    ac = col[:, None] * inv[None, :]
    cr, sr, cc, sc = jnp.cos(ar), jnp.sin(ar), jnp.cos(ac), jnp.sin(ac)
    tabq = jnp.stack([cr.T, sr.T, cc.T, sc.T]).reshape(4, nf, s // Q_TILE, Q_TILE).transpose(2, 0, 1, 3)
    lo = jnp.zeros((s, lane_off), F32)
    hi = jnp.zeros((s, LANES - lane_off - rot_dim), F32)
    cos_k = jnp.concatenate([lo, cr, cr, cc, cc, hi], axis=1)
    sin_k = jnp.concatenate([lo, -sr, sr, -sc, sc, hi], axis=1)
    return tabq, cos_k, sin_k


def _pad_heads_cols(w, heads, d):
    w = w.reshape(w.shape[0], heads, d)
    return jnp.pad(w, ((0, 0), (0, 0), (0, LANES - d))).reshape(w.shape[0], heads * LANES)


def _proj_out(b, s, t, heads, kv_heads, dq, dv):
    specs = [
        pl.BlockSpec((1, heads, t // Q_TILE, dq, Q_TILE), lambda bi, i: (bi, 0, i, 0, 0)),
        pl.BlockSpec((1, kv_heads, t, LANES), lambda bi, i: (bi, 0, i, 0)),
        pl.BlockSpec((1, kv_heads, dv + BF16_ROWS, t), lambda bi, i: (bi, 0, 0, i)),
    ]
    shapes = [
        jax.ShapeDtypeStruct((b, heads, s // Q_TILE, dq, Q_TILE), BF16),
        jax.ShapeDtypeStruct((b, kv_heads, s, LANES), BF16),
        jax.ShapeDtypeStruct((b, kv_heads, dv + BF16_ROWS, s), BF16),
    ]
    return specs, shapes


def _gqa_proj(x, norm_g, w_qkv, k_norm, tabs):
    b, s, d = x.shape
    t = TOKEN_TILE
    nq = GQA_HEADS * GQA_HD
    nk = GQA_KV_HEADS * GQA_HD
    _, cos_k, sin_k = tabs
    wqT = w_qkv[:, :nq].T.astype(BF16)
    wk = _pad_heads_cols(w_qkv[:, nq:nq + nk], GQA_KV_HEADS, GQA_HD).astype(BF16)
    wvT = w_qkv[:, nq + nk:].T.astype(BF16)
    gk = jnp.pad(k_norm, (0, LANES - GQA_HD)).reshape(1, LANES)
    out_specs, out_shape = _proj_out(b, s, t, GQA_HEADS, GQA_KV_HEADS, GQA_HD, GQA_HD)
    return pl.pallas_call(
        _gqa_proj_kernel,
        grid=(b, s // t),
        in_specs=[
            pl.BlockSpec((1, t, d), lambda bi, i: (bi, i, 0)),
            _resident((1, d)),
            _resident((nq, d)),
            _resident((d, GQA_KV_HEADS * LANES)),
            _resident((nk, d)),
            _resident((1, LANES)),
            pl.BlockSpec((t, LANES), lambda bi, i: (i, 0)),
            pl.BlockSpec((t, LANES), lambda bi, i: (i, 0)),
        ],
        out_specs=out_specs,
        out_shape=out_shape,
        compiler_params=_params(40, 2),
        name="gqa_proj",
    )(x, norm_g.reshape(1, d), wqT, wk, wvT, gk, cos_k, sin_k)


def _mla_proj(x, norm_g, w_in, q_lora_norm, w_uq, kv_lora_norm, w_ukv, k_norm, tabs):
    b, s, d = x.shape
    t = TOKEN_TILE
    _, cos_k, sin_k = tabs
    nlat = Q_LORA + KV_LORA
    winT = w_in[:, :nlat].T.astype(BF16)
    w_kr = jnp.pad(w_in[:, nlat:], ((0, 0), (MLA_NOPE, LANES - MLA_QK)))
    wink = jnp.concatenate([w_in[:, Q_LORA:nlat], w_kr], axis=1).astype(BF16)
    wuqT = w_uq.T.astype(BF16)
    wukv = w_ukv.reshape(KV_LORA, MLA_HEADS, MLA_NOPE + MLA_V)
    wuk = _pad_heads_cols(wukv[:, :, :MLA_NOPE].reshape(KV_LORA, MLA_HEADS * MLA_NOPE), MLA_HEADS, MLA_NOPE).astype(BF16)
    wuvT = wukv[:, :, MLA_NOPE:].reshape(KV_LORA, MLA_HEADS * MLA_V).T.astype(BF16)
    gkn = jnp.pad(k_norm[:MLA_NOPE], (0, LANES - MLA_NOPE)).reshape(1, LANES)
    gkr = jnp.pad(k_norm[MLA_NOPE:], (MLA_NOPE, LANES - MLA_QK)).reshape(1, LANES)
    out_specs, out_shape = _proj_out(b, s, t, MLA_HEADS, MLA_HEADS, MLA_QK, MLA_V)
    return pl.pallas_call(
        _mla_proj_kernel,
        grid=(b, s // t),
        in_specs=[
            pl.BlockSpec((1, t, d), lambda bi, i: (bi, i, 0)),
            _resident((1, d)),
            _resident((nlat, d)),
            _resident((d, KV_LORA + LANES)),
            _resident((Q_LORA, 1)),
            _resident((KV_LORA, 1)),
            _resident((1, KV_LORA)),
            _resident((MLA_HEADS * MLA_QK, Q_LORA)),
            _resident((MLA_HEADS * MLA_V, KV_LORA)),
            _resident((KV_LORA, MLA_HEADS * LANES)),
            _resident((1, LANES)),
            _resident((1, LANES)),
            pl.BlockSpec((t, LANES), lambda bi, i: (i, 0)),
            pl.BlockSpec((t, LANES), lambda bi, i: (i, 0)),
        ],
        out_specs=out_specs,
        out_shape=out_shape,
        compiler_params=_params(48, 2),
        name="mla_proj",
    )(x, norm_g.reshape(1, d), winT, wink, q_lora_norm.reshape(Q_LORA, 1), kv_lora_norm.reshape(KV_LORA, 1),
      kv_lora_norm.reshape(1, KV_LORA), wuqT, wuvT, wuk, gkn, gkr, cos_k, sin_k)


def _attn_head_block(heads, kv_heads, nt, dq, tq, s, dva):
    hb = heads
    while hb > heads // kv_heads:
        nk = hb * kv_heads // heads
        block_bytes = 2 * (hb * nt * dq * tq + nk * s * LANES + nk * dva * s + nt * hb * (dva - BF16_ROWS) * tq)
        if 2 * block_bytes <= ATTN_BLOCK_BUDGET:
            break
        hb //= 2
    return hb


def _attention(qT, k, vT, q_gain, tabq, nope):
    b, heads, nt, dq, tq = qT.shape
    kv_heads, s = k.shape[1], k.shape[2]
    dva = vT.shape[2]
    dv = dva - BF16_ROWS
    hb = _attn_head_block(heads, kv_heads, nt, dq, tq, s, dva)
    nk = hb * kv_heads // heads
    nf = tabq.shape[2]
    return pl.pallas_call(
        functools.partial(_attn_kernel, nope=nope),
        grid=(b, heads // hb),
        in_specs=[
            pl.BlockSpec((1, hb, nt, dq, tq), lambda bi, g: (bi, g, 0, 0, 0)),
            pl.BlockSpec((1, nk, s, LANES), lambda bi, g: (bi, g, 0, 0)),
            pl.BlockSpec((1, nk, dva, s), lambda bi, g: (bi, g, 0, 0)),
            _resident((dq, 1)),
            _resident((nt, 4, nf, tq)),
        ],
        out_specs=pl.BlockSpec((1, nt, hb * dv, tq), lambda bi, g: (bi, 0, g, 0)),
        out_shape=jax.ShapeDtypeStruct((b, nt, heads * dv, tq), BF16),
        scratch_shapes=[pltpu.VMEM((LANES, tq), BF16), pltpu.VMEM((s, tq), F32), pltpu.VMEM((s, tq), BF16)],
        compiler_params=_params(48, 2),
        name="attention",
    )(qT, k, vT, q_gain.reshape(dq, 1), tabq)


def _out_ffn(x, oT, w_o, ffn_g, w_gate_up, w_down):
    b, s, d = x.shape
    t = TOKEN_TILE
    dff = w_down.shape[0]
    tq = oT.shape[3]
    wg = w_gate_up[:, :dff].astype(BF16)
    wu = w_gate_up[:, dff:].astype(BF16)
    return pl.pallas_call(
        _out_ffn_kernel,
        grid=(b, s // t),
        in_specs=[
            pl.BlockSpec((1, t, d), lambda bi, i: (bi, i, 0)),
            pl.BlockSpec((1, t // tq, oT.shape[2], tq), lambda bi, i: (bi, i, 0, 0)),
            _resident(w_o.shape),
            _resident((1, d)),
            _resident((d, dff)),
            _resident((d, dff)),
            _resident((dff, d)),
        ],
        out_specs=pl.BlockSpec((1, t, d), lambda bi, i: (bi, i, 0)),
        out_shape=jax.ShapeDtypeStruct((b, s, d), F32),
        scratch_shapes=[pltpu.VMEM((t, dff), BF16)],
        compiler_params=_params(56, 2),
        name="out_ffn",
    )(x, oT, w_o.astype(BF16), ffn_g.reshape(1, d), wg, wu, w_down.astype(BF16))


def kernel(x, mla_norm, mla_w_in, mla_q_lora_norm, mla_w_uq, mla_kv_lora_norm, mla_w_ukv, mla_q_norm, mla_k_norm, mla_w_o, gqa_norm, gqa_w_qkv, gqa_q_norm, gqa_k_norm, gqa_w_o, ffn_norm, ffn_w_gate_up, ffn_w_down):
    s = x.shape[1]
    mla_tabs = _rope_tables(s, MLA_ROPE, MLA_NOPE)
    gqa_tabs = _rope_tables(s, GQA_HD, 0)
    depth = ffn_norm.shape[0]
    for i in range(depth):
        j = i // 2
        if i % 2 == 0:
            qT, k, vT = _mla_proj(x, mla_norm[j], mla_w_in[j], mla_q_lora_norm[j], mla_w_uq[j],
                                  mla_kv_lora_norm[j], mla_w_ukv[j], mla_k_norm[j], mla_tabs)
            oT = _attention(qT, k, vT, mla_q_norm[j] * (MLA_QK ** -0.5 * LOG2E), mla_tabs[0], MLA_NOPE)
            w_o = mla_w_o[j]
        else:
            qT, k, vT = _gqa_proj(x, gqa_norm[j], gqa_w_qkv[j], gqa_k_norm[j], gqa_tabs)
            oT = _attention(qT, k, vT, gqa_q_norm[j] * (GQA_HD ** -0.5 * LOG2E), gqa_tabs[0], 0)
            w_o = gqa_w_o[j]
        x = _out_ffn(x, oT, w_o, ffn_norm[i], ffn_w_gate_up[i], ffn_w_down[i])
    return x
```

```python
import functools
import math

import jax
import jax.numpy as jnp
from jax import lax
from jax.experimental import pallas as pl
from jax.experimental.pallas import tpu as pltpu

GRID_W = 64
ROPE_THETA = 10000.0
EPS = 1e-6

MLA_HEADS = 16
MLA_NOPE = 64
MLA_ROPE = 32
MLA_QK = MLA_NOPE + MLA_ROPE
MLA_V = 64
Q_LORA = 384
KV_LORA = 256

GQA_HEADS = 16
GQA_KV_HEADS = 4
GQA_HD = 64

LANES = 128
BF16_ROWS = 16
TOKEN_TILE = 512
FFN_TOKEN_TILE = 1024
GQA_SUB_TILE = 256
MLA_SUB_TILE = 256
Q_TILE = 512
FF_CHUNK = 256
MIB = 1024 * 1024
ATTN_BLOCK_BUDGET = 36 * MIB
LOG2E = math.log2(math.e)

F32 = jnp.float32
BF16 = jnp.bfloat16

_NT = (((1,), (1,)), ((), ()))
_TN = (((0,), (0,)), ((), ()))


def _rms_tok(x, g):
    return x * lax.rsqrt(jnp.mean(x * x, axis=-1, keepdims=True) + EPS) * g


def _rms_feat(x, g):
    return x * lax.rsqrt(jnp.mean(x * x, axis=0, keepdims=True) + EPS) * g


def _rope_feat(x, tab, nf):
    cr, sr, cc, sc = tab[0], tab[1], tab[2], tab[3]
    a, b, c, d = (x[i * nf:(i + 1) * nf] for i in range(4))
    return jnp.concatenate([a * cr - b * sr, b * cr + a * sr, c * cc - d * sc, d * cc + c * sc], axis=0)


def _rope_tok(x, cos, sin_signed, nf):
    lane = lax.broadcasted_iota(jnp.int32, x.shape, 1)
    first = (lane % (2 * nf)) < nf
    partner = jnp.where(first, pltpu.roll(x, LANES - nf, 1), pltpu.roll(x, nf, 1))
    return x * cos + partner * sin_signed


def _head_block_rms(blk, real, g):
    ss = jnp.sum(blk * blk, axis=-1, keepdims=True)
    return blk * lax.rsqrt(ss * (1.0 / real) + EPS) * g


def _store_q(qT_ref, qT, heads, tok):
    q3 = qT.reshape(heads, qT.shape[0] // heads, qT.shape[1]).astype(BF16)
    c, off = tok.start // Q_TILE, tok.start % Q_TILE
    assert off + q3.shape[2] <= Q_TILE, "a token pass must stay inside one query tile"
    qT_ref[0, :, c, :, off:off + q3.shape[2]] = q3


def _store_v_with_ones(vT_ref, v3, tok):
    heads, _, n = v3.shape
    row = lax.broadcasted_iota(jnp.int32, (heads, BF16_ROWS, n), 1)
    ones_row = jnp.where(row == 0, 1.0, 0.0).astype(F32)
    vT_ref[0, :, :, tok] = jnp.concatenate([v3, ones_row], axis=1).astype(BF16)


def _sub_tiles(t, sub):
    return [slice(c * sub, (c + 1) * sub) for c in range(t // sub)]


def _gqa_proj_kernel(x_ref, g_ref, wqT_ref, wk_ref, wvT_ref, gk_ref, ck_ref, sk_ref, qT_ref, k_ref, vT_ref):
    for tok in _sub_tiles(x_ref.shape[1], GQA_SUB_TILE):
        n = tok.stop - tok.start
        h = _rms_tok(x_ref[0, tok], g_ref[...]).astype(BF16)
        qT = lax.dot_general(wqT_ref[...], h, _NT, preferred_element_type=F32)
        vT = lax.dot_general(wvT_ref[...], h, _NT, preferred_element_type=F32)
        kp = jnp.dot(h, wk_ref[...], preferred_element_type=F32)

        _store_q(qT_ref, qT, GQA_HEADS, tok)
        _store_v_with_ones(vT_ref, vT.reshape(GQA_KV_HEADS, GQA_HD, n), tok)

        cos, sin_signed = ck_ref[tok], sk_ref[tok]
        for j in range(GQA_KV_HEADS):
            blk = _head_block_rms(kp[:, j * LANES:(j + 1) * LANES], GQA_HD, gk_ref[...])
            k_ref[0, j, tok] = _rope_tok(blk, cos, sin_signed, GQA_HD // 4).astype(BF16)


def _mla_proj_kernel(x_ref, g_ref, winT_ref, wink_ref, gqlT_ref, gkvlT_ref, gkvl_ref, wuqT_ref, wuvT_ref,
                     wuk_ref, gkn_ref, gkr_ref, ck_ref, sk_ref, qT_ref, k_ref, vT_ref):
    t = x_ref.shape[1]
    h = _rms_tok(x_ref[0], g_ref[...]).astype(BF16)

    latT = lax.dot_general(winT_ref[...], h, _NT, preferred_element_type=F32)
    cqT = _rms_feat(latT[:Q_LORA], gqlT_ref[...]).astype(BF16)
    ckvT = _rms_feat(latT[Q_LORA:], gkvlT_ref[...]).astype(BF16)
    _store_q(qT_ref, jnp.dot(wuqT_ref[...], cqT, preferred_element_type=F32), MLA_HEADS, slice(0, t))
    vT = jnp.dot(wuvT_ref[...], ckvT, preferred_element_type=F32)
    _store_v_with_ones(vT_ref, vT.reshape(MLA_HEADS, MLA_V, t), slice(0, t))

    for tok in _sub_tiles(t, MLA_SUB_TILE):
        latk = jnp.dot(h[tok], wink_ref[...], preferred_element_type=F32)
        ckv = _rms_tok(latk[:, :KV_LORA], gkvl_ref[...]).astype(BF16)
        kr = _head_block_rms(latk[:, KV_LORA:], MLA_ROPE, gkr_ref[...])
        kr = _rope_tok(kr, ck_ref[tok], sk_ref[tok], MLA_ROPE // 4)
        knp = jnp.dot(ckv, wuk_ref[...], preferred_element_type=F32)
        for j in range(MLA_HEADS):
            blk = _head_block_rms(knp[:, j * LANES:(j + 1) * LANES], MLA_NOPE, gkn_ref[...])
            k_ref[0, j, tok] = (blk + kr).astype(BF16)


def _attn_kernel(qT_ref, k_ref, vT_ref, gq_ref, tab_ref, oT_ref, qn_ref, s_ref, p_ref, *, nope):
    hb, nt, dq, tq = qT_ref.shape[1:]
    nk = k_ref.shape[1]
    nf = tab_ref.shape[2]
    dv = oT_ref.shape[2] // hb
    heads_per_kv = hb // nk
    n_items = hb * nt

    def prep(n):
        h, i = n // nt, n % nt
        q = qT_ref[0, h, i].astype(F32)
        g = gq_ref[...]
        parts = [_rms_feat(q[:nope], g[:nope])] if nope else []
        parts.append(_rope_feat(_rms_feat(q[nope:], g[nope:]), tab_ref[i], nf))
        parts.append(jnp.zeros((LANES - dq, tq), F32))
        qn_ref[...] = jnp.concatenate(parts, axis=0).astype(BF16)

    def scores(n):
        s = jnp.dot(k_ref[0, (n // nt) // heads_per_kv], qn_ref[...], preferred_element_type=F32)
        s_ref[:, :tq] = s
        return jnp.max(s, axis=0, keepdims=True)

    def probs(m):
        p_ref[:, :tq] = jnp.exp2(s_ref[:, :tq] - m).astype(BF16)

    def output(n):
        h, i = n // nt, n % nt
        o = jnp.dot(vT_ref[0, h // heads_per_kv], p_ref[:, :tq], preferred_element_type=F32)
        out = (o[:dv] * (1.0 / o[dv:dv + 1])).astype(BF16)
        row0 = h * dv if isinstance(h, int) else pl.multiple_of(h * dv, dv)
        oT_ref[0, i, pl.ds(row0, dv), :] = out

    prep(0)
    m = scores(0)
    prep(1)
    probs(m)
    m = scores(1)
    prep(2)

    def body(n, m):
        output(n - 1)
        probs(m)
        m = scores(n + 1)
        prep(jnp.minimum(n + 2, n_items - 1))
        return m

    m = lax.fori_loop(1, n_items - 1, body, m)
    output(n_items - 2)
    probs(m)
    output(n_items - 1)


def _out_ffn_kernel(x_ref, oT_ref, wo_ref, gf_ref, wg_ref, wu_ref, wd_ref, out_ref, act_ref):
    y = [lax.dot_general(oT_ref[0, c], wo_ref[...], _TN, preferred_element_type=F32) for c in range(oT_ref.shape[1])]
    x1 = x_ref[0] + (y[0] if len(y) == 1 else jnp.concatenate(y, axis=0))
    h = _rms_tok(x1, gf_ref[...]).astype(BF16)
    for c in range(act_ref.shape[1] // FF_CHUNK):
        sl = slice(c * FF_CHUNK, (c + 1) * FF_CHUNK)
        g = jnp.dot(h, wg_ref[:, sl], preferred_element_type=F32)
        u = jnp.dot(h, wu_ref[:, sl], preferred_element_type=F32)
        act_ref[:, sl] = (g / (1.0 + jnp.exp(-g)) * u).astype(BF16)
    out_ref[0] = x1 + jnp.dot(act_ref[...], wd_ref[...], preferred_element_type=F32)


def _resident(shape):
    return pl.BlockSpec(shape, lambda *_: (0,) * len(shape), pipeline_mode=pl.Buffered(1))


def _params(vmem_mib, ndims):
    return pltpu.CompilerParams(dimension_semantics=("parallel",) * ndims, vmem_limit_bytes=vmem_mib * MIB)


def _rope_tables(s, rot_dim, lane_off):
    half = rot_dim // 2
    nf = half // 2
    tok = jnp.arange(s, dtype=jnp.int32)
    row = (tok // GRID_W).astype(F32)
    col = (tok % GRID_W).astype(F32)
    inv = ROPE_THETA ** (-jnp.arange(0, half, 2, dtype=F32) / half)
    ar = row[:, None] * inv[None, :]
    ac = col[:, None] * inv[None, :]
    cr, sr, cc, sc = jnp.cos(ar), jnp.sin(ar), jnp.cos(ac), jnp.sin(ac)
    tabq = jnp.stack([cr.T, sr.T, cc.T, sc.T]).reshape(4, nf, s // Q_TILE, Q_TILE).transpose(2, 0, 1, 3)
    lo = jnp.zeros((s, lane_off), F32)
    hi = jnp.zeros((s, LANES - lane_off - rot_dim), F32)
    cos_k = jnp.concatenate([lo, cr, cr, cc, cc, hi], axis=1)
    sin_k = jnp.concatenate([lo, -sr, sr, -sc, sc, hi], axis=1)
    return tabq, cos_k, sin_k


def _pad_heads_cols(w, heads, d):
    w = w.reshape(w.shape[0], heads, d)
    return jnp.pad(w, ((0, 0), (0, 0), (0, LANES - d))).reshape(w.shape[0], heads * LANES)


def _proj_out(b, s, t, heads, kv_heads, dq, dv):
    specs = [
        pl.BlockSpec((1, heads, t // Q_TILE, dq, Q_TILE), lambda bi, i: (bi, 0, i, 0, 0)),
        pl.BlockSpec((1, kv_heads, t, LANES), lambda bi, i: (bi, 0, i, 0)),
        pl.BlockSpec((1, kv_heads, dv + BF16_ROWS, t), lambda bi, i: (bi, 0, 0, i)),
    ]
    shapes = [
        jax.ShapeDtypeStruct((b, heads, s // Q_TILE, dq, Q_TILE), BF16),
        jax.ShapeDtypeStruct((b, kv_heads, s, LANES), BF16),
        jax.ShapeDtypeStruct((b, kv_heads, dv + BF16_ROWS, s), BF16),
    ]
    return specs, shapes


def _gqa_proj(x, norm_g, w_qkv, k_norm, tabs):
    b, s, d = x.shape
    t = TOKEN_TILE
    nq = GQA_HEADS * GQA_HD
    nk = GQA_KV_HEADS * GQA_HD
    _, cos_k, sin_k = tabs
    wqT = w_qkv[:, :nq].T.astype(BF16)
    wk = _pad_heads_cols(w_qkv[:, nq:nq + nk], GQA_KV_HEADS, GQA_HD).astype(BF16)
    wvT = w_qkv[:, nq + nk:].T.astype(BF16)
    gk = jnp.pad(k_norm, (0, LANES - GQA_HD)).reshape(1, LANES)
    out_specs, out_shape = _proj_out(b, s, t, GQA_HEADS, GQA_KV_HEADS, GQA_HD, GQA_HD)
    return pl.pallas_call(
        _gqa_proj_kernel,
        grid=(b, s // t),
        in_specs=[
            pl.BlockSpec((1, t, d), lambda bi, i: (bi, i, 0)),
            _resident((1, d)),
            _resident((nq, d)),
            _resident((d, GQA_KV_HEADS * LANES)),
            _resident((nk, d)),
            _resident((1, LANES)),
            pl.BlockSpec((t, LANES), lambda bi, i: (i, 0)),
            pl.BlockSpec((t, LANES), lambda bi, i: (i, 0)),
        ],
        out_specs=out_specs,
        out_shape=out_shape,
        compiler_params=_params(40, 2),
        name="gqa_proj",
    )(x, norm_g.reshape(1, d), wqT, wk, wvT, gk, cos_k, sin_k)


def _mla_proj(x, norm_g, w_in, q_lora_norm, w_uq, kv_lora_norm, w_ukv, k_norm, tabs):
    b, s, d = x.shape
    t = TOKEN_TILE
    _, cos_k, sin_k = tabs
    nlat = Q_LORA + KV_LORA
    winT = w_in[:, :nlat].T.astype(BF16)
    w_kr = jnp.pad(w_in[:, nlat:], ((0, 0), (MLA_NOPE, LANES - MLA_QK)))
    wink = jnp.concatenate([w_in[:, Q_LORA:nlat], w_kr], axis=1).astype(BF16)
    wuqT = w_uq.T.astype(BF16)
    wukv = w_ukv.reshape(KV_LORA, MLA_HEADS, MLA_NOPE + MLA_V)
    wuk = _pad_heads_cols(wukv[:, :, :MLA_NOPE].reshape(KV_LORA, MLA_HEADS * MLA_NOPE), MLA_HEADS, MLA_NOPE).astype(BF16)
    wuvT = wukv[:, :, MLA_NOPE:].reshape(KV_LORA, MLA_HEADS * MLA_V).T.astype(BF16)
    gkn = jnp.pad(k_norm[:MLA_NOPE], (0, LANES - MLA_NOPE)).reshape(1, LANES)
    gkr = jnp.pad(k_norm[MLA_NOPE:], (MLA_NOPE, LANES - MLA_QK)).reshape(1, LANES)
    out_specs, out_shape = _proj_out(b, s, t, MLA_HEADS, MLA_HEADS, MLA_QK, MLA_V)
    return pl.pallas_call(
        _mla_proj_kernel,
        grid=(b, s // t),
        in_specs=[
            pl.BlockSpec((1, t, d), lambda bi, i: (bi, i, 0)),
            _resident((1, d)),
            _resident((nlat, d)),
            _resident((d, KV_LORA + LANES)),
            _resident((Q_LORA, 1)),
            _resident((KV_LORA, 1)),
            _resident((1, KV_LORA)),
            _resident((MLA_HEADS * MLA_QK, Q_LORA)),
            _resident((MLA_HEADS * MLA_V, KV_LORA)),
            _resident((KV_LORA, MLA_HEADS * LANES)),
            _resident((1, LANES)),
            _resident((1, LANES)),
            pl.BlockSpec((t, LANES), lambda bi, i: (i, 0)),
            pl.BlockSpec((t, LANES), lambda bi, i: (i, 0)),
        ],
        out_specs=out_specs,
        out_shape=out_shape,
        compiler_params=_params(48, 2),
        name="mla_proj",
    )(x, norm_g.reshape(1, d), winT, wink, q_lora_norm.reshape(Q_LORA, 1), kv_lora_norm.reshape(KV_LORA, 1),
      kv_lora_norm.reshape(1, KV_LORA), wuqT, wuvT, wuk, gkn, gkr, cos_k, sin_k)


def _attn_head_block(heads, kv_heads, nt, dq, tq, s, dva):
    hb = heads
    while hb > heads // kv_heads:
        nk = hb * kv_heads // heads
        block_bytes = 2 * (hb * nt * dq * tq + nk * s * LANES + nk * dva * s + nt * hb * (dva - BF16_ROWS) * tq)
        if 2 * block_bytes <= ATTN_BLOCK_BUDGET:
            break
        hb //= 2
    return hb


def _attention(qT, k, vT, q_gain, tabq, nope, pad_lanes):
    b, heads, nt, dq, tq = qT.shape
    kv_heads, s = k.shape[1], k.shape[2]
    dva = vT.shape[2]
    dv = dva - BF16_ROWS
    hb = _attn_head_block(heads, kv_heads, nt, dq, tq, s, dva)
    nk = hb * kv_heads // heads
    nf = tabq.shape[2]
    return pl.pallas_call(
        functools.partial(_attn_kernel, nope=nope),
        grid=(b, heads // hb),
        in_specs=[
            pl.BlockSpec((1, hb, nt, dq, tq), lambda bi, g: (bi, g, 0, 0, 0)),
            pl.BlockSpec((1, nk, s, LANES), lambda bi, g: (bi, g, 0, 0)),
            pl.BlockSpec((1, nk, dva, s), lambda bi, g: (bi, g, 0, 0)),
            _resident((dq, 1)),
            _resident((nt, 4, nf, tq)),
        ],
        out_specs=pl.BlockSpec((1, nt, hb * dv, tq), lambda bi, g: (bi, 0, g, 0)),
        out_shape=jax.ShapeDtypeStruct((b, nt, heads * dv, tq), BF16),
        scratch_shapes=[pltpu.VMEM((LANES, tq), BF16), pltpu.VMEM((s, tq + pad_lanes), F32),
                        pltpu.VMEM((s, tq + pad_lanes), BF16)],
        compiler_params=_params(48, 2),
        name="attention",
    )(qT, k, vT, q_gain.reshape(dq, 1), tabq)


def _out_ffn(x, oT, w_o, ffn_g, w_gate_up, w_down):
    b, s, d = x.shape
    t = FFN_TOKEN_TILE
    dff = w_down.shape[0]
    tq = oT.shape[3]
    wg = w_gate_up[:, :dff].astype(BF16)
    wu = w_gate_up[:, dff:].astype(BF16)
    return pl.pallas_call(
        _out_ffn_kernel,
        grid=(b, s // t),
        in_specs=[
            pl.BlockSpec((1, t, d), lambda bi, i: (bi, i, 0)),
            pl.BlockSpec((1, t // tq, oT.shape[2], tq), lambda bi, i: (bi, i, 0, 0)),
            _resident(w_o.shape),
            _resident((1, d)),
            _resident((d, dff)),
            _resident((d, dff)),
            _resident((dff, d)),
        ],
        out_specs=pl.BlockSpec((1, t, d), lambda bi, i: (bi, i, 0)),
        out_shape=jax.ShapeDtypeStruct((b, s, d), F32),
        scratch_shapes=[pltpu.VMEM((t, dff), BF16)],
        compiler_params=_params(60, 2),
        name="out_ffn",
    )(x, oT, w_o.astype(BF16), ffn_g.reshape(1, d), wg, wu, w_down.astype(BF16))


def kernel(x, mla_norm, mla_w_in, mla_q_lora_norm, mla_w_uq, mla_kv_lora_norm, mla_w_ukv, mla_q_norm, mla_k_norm, mla_w_o, gqa_norm, gqa_w_qkv, gqa_q_norm, gqa_k_norm, gqa_w_o, ffn_norm, ffn_w_gate_up, ffn_w_down):
    s = x.shape[1]
    mla_tabs = _rope_tables(s, MLA_ROPE, MLA_NOPE)
    gqa_tabs = _rope_tables(s, GQA_HD, 0)
    depth = ffn_norm.shape[0]
    for i in range(depth):
        j = i // 2
        if i % 2 == 0:
            qT, k, vT = _mla_proj(x, mla_norm[j], mla_w_in[j], mla_q_lora_norm[j], mla_w_uq[j],
                                  mla_kv_lora_norm[j], mla_w_ukv[j], mla_k_norm[j], mla_tabs)
            oT = _attention(qT, k, vT, mla_q_norm[j] * (MLA_QK ** -0.5 * LOG2E), mla_tabs[0], MLA_NOPE, LANES)
            w_o = mla_w_o[j]
        else:
            qT, k, vT = _gqa_proj(x, gqa_norm[j], gqa_w_qkv[j], gqa_k_norm[j], gqa_tabs)
            oT = _attention(qT, k, vT, gqa_q_norm[j] * (GQA_HD ** -0.5 * LOG2E), gqa_tabs[0], 0, 0)
            w_o = gqa_w_o[j]
        x = _out_ffn(x, oT, w_o, ffn_norm[i], ffn_w_gate_up[i], ffn_w_down[i])
    return x
```

```python
import functools
import math

import jax
import jax.numpy as jnp
from jax import lax
from jax.experimental import pallas as pl
from jax.experimental.pallas import tpu as pltpu

GRID_W = 64
ROPE_THETA = 10000.0
EPS = 1e-6

MLA_HEADS = 16
MLA_NOPE = 64
MLA_ROPE = 32
MLA_QK = MLA_NOPE + MLA_ROPE
MLA_V = 64
Q_LORA = 384
KV_LORA = 256

GQA_HEADS = 16
GQA_KV_HEADS = 4
GQA_HD = 64

LANES = 128
BF16_ROWS = 16
TOKEN_TILE = 512
FFN_TOKEN_TILE = 1024
GQA_SUB_TILE = 256
MLA_SUB_TILE = 256
Q_TILE = 512
FF_CHUNK = 256
MIB = 1024 * 1024
ATTN_BLOCK_BUDGET = 36 * MIB
LOG2E = math.log2(math.e)
MAX_CONST_SHIFT = 60.0
NORM_SLACK = 1.02

F32 = jnp.float32
BF16 = jnp.bfloat16

_NT = (((1,), (1,)), ((), ()))
_TN = (((0,), (0,)), ((), ()))


def _rms_tok(x, g):
    return x * lax.rsqrt(jnp.mean(x * x, axis=-1, keepdims=True) + EPS) * g


def _rms_feat(x, g):
    return x * lax.rsqrt(jnp.mean(x * x, axis=0, keepdims=True) + EPS) * g


def _rope_feat(x, tab, nf):
    cr, sr, cc, sc = tab[0], tab[1], tab[2], tab[3]
    a, b, c, d = (x[i * nf:(i + 1) * nf] for i in range(4))
    return jnp.concatenate([a * cr - b * sr, b * cr + a * sr, c * cc - d * sc, d * cc + c * sc], axis=0)


def _rope_tok(x, cos, sin_signed, nf):
    lane = lax.broadcasted_iota(jnp.int32, x.shape, 1)
    first = (lane % (2 * nf)) < nf
    partner = jnp.where(first, pltpu.roll(x, LANES - nf, 1), pltpu.roll(x, nf, 1))
    return x * cos + partner * sin_signed


def _head_block_rms(blk, real, g):
    ss = jnp.sum(blk * blk, axis=-1, keepdims=True)
    return blk * lax.rsqrt(ss * (1.0 / real) + EPS) * g


def _store_q(qT_ref, qT, heads, tok):
    q3 = qT.reshape(heads, qT.shape[0] // heads, qT.shape[1]).astype(BF16)
    c, off = tok.start // Q_TILE, tok.start % Q_TILE
    assert off + q3.shape[2] <= Q_TILE, "a token pass must stay inside one query tile"
    qT_ref[0, :, c, :, off:off + q3.shape[2]] = q3


def _store_v_with_ones(vT_ref, v3, tok):
    heads, _, n = v3.shape
    row = lax.broadcasted_iota(jnp.int32, (heads, BF16_ROWS, n), 1)
    ones_row = jnp.where(row == 0, 1.0, 0.0).astype(F32)
    vT_ref[0, :, :, tok] = jnp.concatenate([v3, ones_row], axis=1).astype(BF16)


def _sub_tiles(t, sub):
    return [slice(c * sub, (c + 1) * sub) for c in range(t // sub)]


def _gqa_proj_kernel(x_ref, g_ref, wqT_ref, wk_ref, wvT_ref, gk_ref, ck_ref, sk_ref, qT_ref, k_ref, vT_ref):
    for tok in _sub_tiles(x_ref.shape[1], GQA_SUB_TILE):
        n = tok.stop - tok.start
        h = _rms_tok(x_ref[0, tok], g_ref[...]).astype(BF16)
        qT = lax.dot_general(wqT_ref[...], h, _NT, preferred_element_type=F32)
        vT = lax.dot_general(wvT_ref[...], h, _NT, preferred_element_type=F32)
        kp = jnp.dot(h, wk_ref[...], preferred_element_type=F32)

        _store_q(qT_ref, qT, GQA_HEADS, tok)
        _store_v_with_ones(vT_ref, vT.reshape(GQA_KV_HEADS, GQA_HD, n), tok)

        cos, sin_signed = ck_ref[tok], sk_ref[tok]
        for j in range(GQA_KV_HEADS):
            blk = _head_block_rms(kp[:, j * LANES:(j + 1) * LANES], GQA_HD, gk_ref[...])
            k_ref[0, j, tok] = _rope_tok(blk, cos, sin_signed, GQA_HD // 4).astype(BF16)


def _mla_proj_kernel(x_ref, g_ref, winT_ref, wink_ref, gqlT_ref, gkvlT_ref, gkvl_ref, wuqT_ref, wuvT_ref,
                     wuk_ref, gkn_ref, gkr_ref, ck_ref, sk_ref, qT_ref, k_ref, vT_ref):
    t = x_ref.shape[1]
    h = _rms_tok(x_ref[0], g_ref[...]).astype(BF16)

    latT = lax.dot_general(winT_ref[...], h, _NT, preferred_element_type=F32)
    cqT = _rms_feat(latT[:Q_LORA], gqlT_ref[...]).astype(BF16)
    ckvT = _rms_feat(latT[Q_LORA:], gkvlT_ref[...]).astype(BF16)
    _store_q(qT_ref, jnp.dot(wuqT_ref[...], cqT, preferred_element_type=F32), MLA_HEADS, slice(0, t))
    vT = jnp.dot(wuvT_ref[...], ckvT, preferred_element_type=F32)
    _store_v_with_ones(vT_ref, vT.reshape(MLA_HEADS, MLA_V, t), slice(0, t))

    for tok in _sub_tiles(t, MLA_SUB_TILE):
        latk = jnp.dot(h[tok], wink_ref[...], preferred_element_type=F32)
        ckv = _rms_tok(latk[:, :KV_LORA], gkvl_ref[...]).astype(BF16)
        kr = _head_block_rms(latk[:, KV_LORA:], MLA_ROPE, gkr_ref[...])
        kr = _rope_tok(kr, ck_ref[tok], sk_ref[tok], MLA_ROPE // 4)
        knp = jnp.dot(ckv, wuk_ref[...], preferred_element_type=F32)
        for j in range(MLA_HEADS):
            blk = _head_block_rms(knp[:, j * LANES:(j + 1) * LANES], MLA_NOPE, gkn_ref[...])
            k_ref[0, j, tok] = (blk + kr).astype(BF16)


def _prep_q(qT_ref, gq_ref, tab_ref, qn_ref, n, nope):
    nt, dq, tq = qT_ref.shape[2:]
    nf = tab_ref.shape[2]
    h, i = n // nt, n % nt
    q = qT_ref[0, h, i].astype(F32)
    g = gq_ref[...]
    parts = [_rms_feat(q[:nope], g[:nope])] if nope else []
    parts.append(_rope_feat(_rms_feat(q[nope:], g[nope:]), tab_ref[i], nf))
    parts.append(jnp.zeros((LANES - dq, tq), F32))
    qn_ref[...] = jnp.concatenate(parts, axis=0).astype(BF16)


def _store_out(oT_ref, vT_ref, p, n, heads_per_kv):
    nt = oT_ref.shape[1]
    dv = vT_ref.shape[2] - BF16_ROWS
    h, i = n // nt, n % nt
    o = jnp.dot(vT_ref[0, h // heads_per_kv], p, preferred_element_type=F32)
    out = (o[:dv] * (1.0 / o[dv:dv + 1])).astype(BF16)
    row0 = h * dv if isinstance(h, int) else pl.multiple_of(h * dv, dv)
    oT_ref[0, i, pl.ds(row0, dv), :] = out


def _attn_kernel(qT_ref, k_ref, vT_ref, gq_ref, tab_ref, oT_ref, qn_ref, s_ref, p_ref, *, nope):
    hb, nt = qT_ref.shape[1:3]
    heads_per_kv = hb // k_ref.shape[1]
    n_items = hb * nt

    def prep(n):
        _prep_q(qT_ref, gq_ref, tab_ref, qn_ref, n, nope)

    def scores(n):
        s = jnp.dot(k_ref[0, (n // nt) // heads_per_kv], qn_ref[...], preferred_element_type=F32)
        s_ref[...] = s
        return jnp.max(s, axis=0, keepdims=True)

    def probs(m):
        p_ref[...] = jnp.exp2(s_ref[...] - m).astype(BF16)

    def output(n):
        _store_out(oT_ref, vT_ref, p_ref[...], n, heads_per_kv)

    prep(0)
    m = scores(0)
    prep(1)
    probs(m)
    m = scores(1)
    prep(2)

    def body(n, m):
        output(n - 1)
        probs(m)
        m = scores(n + 1)
        prep(jnp.minimum(n + 2, n_items - 1))
        return m

    m = lax.fori_loop(1, n_items - 1, body, m)
    output(n_items - 2)
    probs(m)
    output(n_items - 1)


def _attn_const_shift_kernel(qT_ref, k_ref, vT_ref, gq_ref, tab_ref, shift_ref, oT_ref, qn_ref, p_ref, *, nope):
    hb, nt = qT_ref.shape[1:3]
    heads_per_kv = hb // k_ref.shape[1]
    n_items = hb * nt
    shift = shift_ref[...]

    def prep(n):
        _prep_q(qT_ref, gq_ref, tab_ref, qn_ref, n, nope)

    def probs(n):
        s = jnp.dot(k_ref[0, (n // nt) // heads_per_kv], qn_ref[...], preferred_element_type=F32)
        p_ref[...] = jnp.exp2(s - shift).astype(BF16)

    def output(n):
        _store_out(oT_ref, vT_ref, p_ref[...], n, heads_per_kv)

    prep(0)
    probs(0)
    prep(1)

    def body(n, carry):
        output(n - 1)
        probs(n)
        prep(jnp.minimum(n + 1, n_items - 1))
        return carry

    lax.fori_loop(1, n_items, body, 0)
    output(n_items - 1)


def _out_ffn_kernel(x_ref, oT_ref, wo_ref, gf_ref, wg_ref, wu_ref, wd_ref, out_ref, act_ref):
    y = [lax.dot_general(oT_ref[0, c], wo_ref[...], _TN, preferred_element_type=F32) for c in range(oT_ref.shape[1])]
    x1 = x_ref[0] + (y[0] if len(y) == 1 else jnp.concatenate(y, axis=0))
    h = _rms_tok(x1, gf_ref[...]).astype(BF16)
    for c in range(act_ref.shape[1] // FF_CHUNK):
        sl = slice(c * FF_CHUNK, (c + 1) * FF_CHUNK)
        g = jnp.dot(h, wg_ref[:, sl], preferred_element_type=F32)
        u = jnp.dot(h, wu_ref[:, sl], preferred_element_type=F32)
        act_ref[:, sl] = (g / (1.0 + jnp.exp(-g)) * u).astype(BF16)
    out_ref[0] = x1 + jnp.dot(act_ref[...], wd_ref[...], preferred_element_type=F32)


def _resident(shape):
    return pl.BlockSpec(shape, lambda *_: (0,) * len(shape), pipeline_mode=pl.Buffered(1))


def _params(vmem_mib, ndims):
    return pltpu.CompilerParams(dimension_semantics=("parallel",) * ndims, vmem_limit_bytes=vmem_mib * MIB)


def _rope_tables(s, rot_dim, lane_off):
    half = rot_dim // 2
    nf = half // 2
    tok = jnp.arange(s, dtype=jnp.int32)
    row = (tok // GRID_W).astype(F32)
    col = (tok % GRID_W).astype(F32)
    inv = ROPE_THETA ** (-jnp.arange(0, half, 2, dtype=F32) / half)
    ar = row[:, None] * inv[None, :]
    ac = col[:, None] * inv[None, :]
    cr, sr, cc, sc = jnp.cos(ar), jnp.sin(ar), jnp.cos(ac), jnp.sin(ac)
    tabq = jnp.stack([cr.T, sr.T, cc.T, sc.T]).reshape(4, nf, s // Q_TILE, Q_TILE).transpose(2, 0, 1, 3)
    lo = jnp.zeros((s, lane_off), F32)
    hi = jnp.zeros((s, LANES - lane_off - rot_dim), F32)
    cos_k = jnp.concatenate([lo, cr, cr, cc, cc, hi], axis=1)
    sin_k = jnp.concatenate([lo, -sr, sr, -sc, sc, hi], axis=1)
    return tabq, cos_k, sin_k


def _pad_heads_cols(w, heads, d):
    w = w.reshape(w.shape[0], heads, d)
    return jnp.pad(w, ((0, 0), (0, 0), (0, LANES - d))).reshape(w.shape[0], heads * LANES)


def _proj_out(b, s, t, heads, kv_heads, dq, dv):
    specs = [
        pl.BlockSpec((1, heads, t // Q_TILE, dq, Q_TILE), lambda bi, i: (bi, 0, i, 0, 0)),
        pl.BlockSpec((1, kv_heads, t, LANES), lambda bi, i: (bi, 0, i, 0)),
        pl.BlockSpec((1, kv_heads, dv + BF16_ROWS, t), lambda bi, i: (bi, 0, 0, i)),
    ]
    shapes = [
        jax.ShapeDtypeStruct((b, heads, s // Q_TILE, dq, Q_TILE), BF16),
        jax.ShapeDtypeStruct((b, kv_heads, s, LANES), BF16),
        jax.ShapeDtypeStruct((b, kv_heads, dv + BF16_ROWS, s), BF16),
    ]
    return specs, shapes


def _gqa_proj(x, norm_g, w_qkv, k_norm, tabs):
    b, s, d = x.shape
    t = TOKEN_TILE
    nq = GQA_HEADS * GQA_HD
    nk = GQA_KV_HEADS * GQA_HD
    _, cos_k, sin_k = tabs
    wqT = w_qkv[:, :nq].T.astype(BF16)
    wk = _pad_heads_cols(w_qkv[:, nq:nq + nk], GQA_KV_HEADS, GQA_HD).astype(BF16)
    wvT = w_qkv[:, nq + nk:].T.astype(BF16)
    gk = jnp.pad(k_norm, (0, LANES - GQA_HD)).reshape(1, LANES)
    out_specs, out_shape = _proj_out(b, s, t, GQA_HEADS, GQA_KV_HEADS, GQA_HD, GQA_HD)
    return pl.pallas_call(
        _gqa_proj_kernel,
        grid=(b, s // t),
        in_specs=[
            pl.BlockSpec((1, t, d), lambda bi, i: (bi, i, 0)),
            _resident((1, d)),
            _resident((nq, d)),
            _resident((d, GQA_KV_HEADS * LANES)),
            _resident((nk, d)),
            _resident((1, LANES)),
            pl.BlockSpec((t, LANES), lambda bi, i: (i, 0)),
            pl.BlockSpec((t, LANES), lambda bi, i: (i, 0)),
        ],
        out_specs=out_specs,
        out_shape=out_shape,
        compiler_params=_params(40, 2),
        name="gqa_proj",
    )(x, norm_g.reshape(1, d), wqT, wk, wvT, gk, cos_k, sin_k)


def _mla_proj(x, norm_g, w_in, q_lora_norm, w_uq, kv_lora_norm, w_ukv, k_norm, tabs):
    b, s, d = x.shape
    t = TOKEN_TILE
    _, cos_k, sin_k = tabs
    nlat = Q_LORA + KV_LORA
    winT = w_in[:, :nlat].T.astype(BF16)
    w_kr = jnp.pad(w_in[:, nlat:], ((0, 0), (MLA_NOPE, LANES - MLA_QK)))
    wink = jnp.concatenate([w_in[:, Q_LORA:nlat], w_kr], axis=1).astype(BF16)
    wuqT = w_uq.T.astype(BF16)
    wukv = w_ukv.reshape(KV_LORA, MLA_HEADS, MLA_NOPE + MLA_V)
    wuk = _pad_heads_cols(wukv[:, :, :MLA_NOPE].reshape(KV_LORA, MLA_HEADS * MLA_NOPE), MLA_HEADS, MLA_NOPE).astype(BF16)
    wuvT = wukv[:, :, MLA_NOPE:].reshape(KV_LORA, MLA_HEADS * MLA_V).T.astype(BF16)
    gkn = jnp.pad(k_norm[:MLA_NOPE], (0, LANES - MLA_NOPE)).reshape(1, LANES)
    gkr = jnp.pad(k_norm[MLA_NOPE:], (MLA_NOPE, LANES - MLA_QK)).reshape(1, LANES)
    out_specs, out_shape = _proj_out(b, s, t, MLA_HEADS, MLA_HEADS, MLA_QK, MLA_V)
    return pl.pallas_call(
        _mla_proj_kernel,
        grid=(b, s // t),
        in_specs=[
            pl.BlockSpec((1, t, d), lambda bi, i: (bi, i, 0)),
            _resident((1, d)),
            _resident((nlat, d)),
            _resident((d, KV_LORA + LANES)),
            _resident((Q_LORA, 1)),
            _resident((KV_LORA, 1)),
            _resident((1, KV_LORA)),
            _resident((MLA_HEADS * MLA_QK, Q_LORA)),
            _resident((MLA_HEADS * MLA_V, KV_LORA)),
            _resident((KV_LORA, MLA_HEADS * LANES)),
            _resident((1, LANES)),
            _resident((1, LANES)),
            pl.BlockSpec((t, LANES), lambda bi, i: (i, 0)),
            pl.BlockSpec((t, LANES), lambda bi, i: (i, 0)),
        ],
        out_specs=out_specs,
        out_shape=out_shape,
        compiler_params=_params(48, 2),
        name="mla_proj",
    )(x, norm_g.reshape(1, d), winT, wink, q_lora_norm.reshape(Q_LORA, 1), kv_lora_norm.reshape(KV_LORA, 1),
      kv_lora_norm.reshape(1, KV_LORA), wuqT, wuvT, wuk, gkn, gkr, cos_k, sin_k)


def _attn_head_block(heads, kv_heads, nt, dq, tq, s, dva):
    hb = heads
    while hb > heads // kv_heads:
        nk = hb * kv_heads // heads
        block_bytes = 2 * (hb * nt * dq * tq + nk * s * LANES + nk * dva * s + nt * hb * (dva - BF16_ROWS) * tq)
        if 2 * block_bytes <= ATTN_BLOCK_BUDGET:
            break
        hb //= 2
    return hb


def _score_bound(q_gain, k_gain, nope):
    def norm_bound(g):
        parts = [g[:nope], g[nope:]] if nope else [g]
        return jnp.sqrt(sum(p.shape[0] * jnp.max(jnp.abs(p)) ** 2 for p in parts))
    return NORM_SLACK * NORM_SLACK * norm_bound(q_gain) * norm_bound(k_gain)


def _attention(qT, k, vT, q_gain, k_gain, tabq, nope):
    b, heads, nt, dq, tq = qT.shape
    kv_heads, s = k.shape[1], k.shape[2]
    dva = vT.shape[2]
    dv = dva - BF16_ROWS
    hb = _attn_head_block(heads, kv_heads, nt, dq, tq, s, dva)
    nk = hb * kv_heads // heads
    nf = tabq.shape[2]
    in_specs = [
        pl.BlockSpec((1, hb, nt, dq, tq), lambda bi, g: (bi, g, 0, 0, 0)),
        pl.BlockSpec((1, nk, s, LANES), lambda bi, g: (bi, g, 0, 0)),
        pl.BlockSpec((1, nk, dva, s), lambda bi, g: (bi, g, 0, 0)),
        _resident((dq, 1)),
        _resident((nt, 4, nf, tq)),
    ]
    common = dict(
        grid=(b, heads // hb),
        out_specs=pl.BlockSpec((1, nt, hb * dv, tq), lambda bi, g: (bi, 0, g, 0)),
        out_shape=jax.ShapeDtypeStruct((b, nt, heads * dv, tq), BF16),
        compiler_params=_params(48, 2),
    )
    q_scratch, p_scratch = pltpu.VMEM((LANES, tq), BF16), pltpu.VMEM((s, tq), BF16)
    gq = q_gain.reshape(dq, 1)
    shift = _score_bound(q_gain, k_gain, nope)

    def const_shift(shift):
        return pl.pallas_call(
            functools.partial(_attn_const_shift_kernel, nope=nope),
            in_specs=in_specs + [_resident((1, 1))],
            scratch_shapes=[q_scratch, p_scratch],
            name="attention_const_shift", **common,
        )(qT, k, vT, gq, tabq, shift.reshape(1, 1))

    def exact_max(shift):
        del shift
        return pl.pallas_call(
            functools.partial(_attn_kernel, nope=nope),
            in_specs=in_specs,
            scratch_shapes=[q_scratch, pltpu.VMEM((s, tq), F32), p_scratch],
            name="attention_exact_max", **common,
        )(qT, k, vT, gq, tabq)

    return lax.cond(shift <= MAX_CONST_SHIFT, const_shift, exact_max, shift)


def _out_ffn(x, oT, w_o, ffn_g, w_gate_up, w_down):
    b, s, d = x.shape
    t = FFN_TOKEN_TILE
    dff = w_down.shape[0]
    tq = oT.shape[3]
    wg = w_gate_up[:, :dff].astype(BF16)
    wu = w_gate_up[:, dff:].astype(BF16)
    return pl.pallas_call(
        _out_ffn_kernel,
        grid=(b, s // t),
        in_specs=[
            pl.BlockSpec((1, t, d), lambda bi, i: (bi, i, 0)),
            pl.BlockSpec((1, t // tq, oT.shape[2], tq), lambda bi, i: (bi, i, 0, 0)),
            _resident(w_o.shape),
            _resident((1, d)),
            _resident((d, dff)),
            _resident((d, dff)),
            _resident((dff, d)),
        ],
        out_specs=pl.BlockSpec((1, t, d), lambda bi, i: (bi, i, 0)),
        out_shape=jax.ShapeDtypeStruct((b, s, d), F32),
        scratch_shapes=[pltpu.VMEM((t, dff), BF16)],
        compiler_params=_params(60, 2),
        name="out_ffn",
    )(x, oT, w_o.astype(BF16), ffn_g.reshape(1, d), wg, wu, w_down.astype(BF16))


def kernel(x, mla_norm, mla_w_in, mla_q_lora_norm, mla_w_uq, mla_kv_lora_norm, mla_w_ukv, mla_q_norm, mla_k_norm, mla_w_o, gqa_norm, gqa_w_qkv, gqa_q_norm, gqa_k_norm, gqa_w_o, ffn_norm, ffn_w_gate_up, ffn_w_down):
    s = x.shape[1]
    mla_tabs = _rope_tables(s, MLA_ROPE, MLA_NOPE)
    gqa_tabs = _rope_tables(s, GQA_HD, 0)
    depth = ffn_norm.shape[0]
    for i in range(depth):
        j = i // 2
        if i % 2 == 0:
            qT, k, vT = _mla_proj(x, mla_norm[j], mla_w_in[j], mla_q_lora_norm[j], mla_w_uq[j],
                                  mla_kv_lora_norm[j], mla_w_ukv[j], mla_k_norm[j], mla_tabs)
            q_gain = mla_q_norm[j] * (MLA_QK ** -0.5 * LOG2E)
            oT = _attention(qT, k, vT, q_gain, mla_k_norm[j], mla_tabs[0], MLA_NOPE)
            w_o = mla_w_o[j]
        else:
            qT, k, vT = _gqa_proj(x, gqa_norm[j], gqa_w_qkv[j], gqa_k_norm[j], gqa_tabs)
            q_gain = gqa_q_norm[j] * (GQA_HD ** -0.5 * LOG2E)
            oT = _attention(qT, k, vT, q_gain, gqa_k_norm[j], gqa_tabs[0], 0)
            w_o = gqa_w_o[j]
        x = _out_ffn(x, oT, w_o, ffn_norm[i], ffn_w_gate_up[i], ffn_w_down[i])
    return x
```

```python
import functools
import math

import jax
import jax.numpy as jnp
from jax import lax
from jax.experimental import pallas as pl
from jax.experimental.pallas import tpu as pltpu

GRID_W = 64
ROPE_THETA = 10000.0
EPS = 1e-6

MLA_HEADS = 16
MLA_NOPE = 64
MLA_ROPE = 32
MLA_QK = MLA_NOPE + MLA_ROPE
MLA_V = 64
Q_LORA = 384
KV_LORA = 256

GQA_HEADS = 16
GQA_KV_HEADS = 4
GQA_HD = 64

LANES = 128
BF16_ROWS = 16
TOKEN_TILE = 512
FFN_TOKEN_TILE = 1024
GQA_SUB_TILE = 256
MLA_SUB_TILE = 256
Q_TILE = 512
FF_CHUNK = 256
MIB = 1024 * 1024
ATTN_BLOCK_BUDGET = 36 * MIB
LOG2E = math.log2(math.e)

F32 = jnp.float32
BF16 = jnp.bfloat16

_NT = (((1,), (1,)), ((), ()))
_TN = (((0,), (0,)), ((), ()))


def _rms_tok(x, g):
    return x * lax.rsqrt(jnp.mean(x * x, axis=-1, keepdims=True) + EPS) * g


def _rms_feat(x, g):
    return x * lax.rsqrt(jnp.mean(x * x, axis=0, keepdims=True) + EPS) * g


def _rope_feat(x, tab, nf):
    cr, sr, cc, sc = tab[0], tab[1], tab[2], tab[3]
    a, b, c, d = (x[i * nf:(i + 1) * nf] for i in range(4))
    return jnp.concatenate([a * cr - b * sr, b * cr + a * sr, c * cc - d * sc, d * cc + c * sc], axis=0)


def _rope_tok(x, cos, sin_signed, nf):
    lane = lax.broadcasted_iota(jnp.int32, x.shape, 1)
    first = (lane % (2 * nf)) < nf
    partner = jnp.where(first, pltpu.roll(x, LANES - nf, 1), pltpu.roll(x, nf, 1))
    return x * cos + partner * sin_signed


def _head_block_rms(blk, real, g):
    ss = jnp.sum(blk * blk, axis=-1, keepdims=True)
    return blk * lax.rsqrt(ss * (1.0 / real) + EPS) * g


def _store_q(qT_ref, qT, heads, tok):
    q3 = qT.reshape(heads, qT.shape[0] // heads, qT.shape[1]).astype(BF16)
    c, off = tok.start // Q_TILE, tok.start % Q_TILE
    assert off + q3.shape[2] <= Q_TILE, "a token pass must stay inside one query tile"
    qT_ref[0, :, c, :, off:off + q3.shape[2]] = q3


def _store_v_with_ones(vT_ref, v3, tok):
    heads, _, n = v3.shape
    row = lax.broadcasted_iota(jnp.int32, (heads, BF16_ROWS, n), 1)
    ones_row = jnp.where(row == 0, 1.0, 0.0).astype(F32)
    vT_ref[0, :, :, tok] = jnp.concatenate([v3, ones_row], axis=1).astype(BF16)


def _sub_tiles(t, sub):
    return [slice(c * sub, (c + 1) * sub) for c in range(t // sub)]


def _gqa_proj_kernel(x_ref, g_ref, wqT_ref, wk_ref, wvT_ref, gk_ref, ck_ref, sk_ref, qT_ref, k_ref, vT_ref):
    for tok in _sub_tiles(x_ref.shape[1], GQA_SUB_TILE):
        n = tok.stop - tok.start
        h = _rms_tok(x_ref[0, tok], g_ref[...]).astype(BF16)
        qT = lax.dot_general(wqT_ref[...], h, _NT, preferred_element_type=F32)
        vT = lax.dot_general(wvT_ref[...], h, _NT, preferred_element_type=F32)
        kp = jnp.dot(h, wk_ref[...], preferred_element_type=F32)

        _store_q(qT_ref, qT, GQA_HEADS, tok)
        _store_v_with_ones(vT_ref, vT.reshape(GQA_KV_HEADS, GQA_HD, n), tok)

        cos, sin_signed = ck_ref[tok], sk_ref[tok]
        for j in range(GQA_KV_HEADS):
            blk = _head_block_rms(kp[:, j * LANES:(j + 1) * LANES], GQA_HD, gk_ref[...])
            k_ref[0, j, tok] = _rope_tok(blk, cos, sin_signed, GQA_HD // 4).astype(BF16)


def _mla_proj_kernel(x_ref, g_ref, winT_ref, wink_ref, gqlT_ref, gkvlT_ref, gkvl_ref, wuqT_ref, wuvT_ref,
                     wuk_ref, gkn_ref, gkr_ref, ck_ref, sk_ref, qT_ref, k_ref, vT_ref):
    t = x_ref.shape[1]
    h = _rms_tok(x_ref[0], g_ref[...]).astype(BF16)

    latT = lax.dot_general(winT_ref[...], h, _NT, preferred_element_type=F32)
    cqT = _rms_feat(latT[:Q_LORA], gqlT_ref[...]).astype(BF16)
    ckvT = _rms_feat(latT[Q_LORA:], gkvlT_ref[...]).astype(BF16)
    _store_q(qT_ref, jnp.dot(wuqT_ref[...], cqT, preferred_element_type=F32), MLA_HEADS, slice(0, t))
    vT = jnp.dot(wuvT_ref[...], ckvT, preferred_element_type=F32)
    _store_v_with_ones(vT_ref, vT.reshape(MLA_HEADS, MLA_V, t), slice(0, t))

    for tok in _sub_tiles(t, MLA_SUB_TILE):
        latk = jnp.dot(h[tok], wink_ref[...], preferred_element_type=F32)
        ckv = _rms_tok(latk[:, :KV_LORA], gkvl_ref[...]).astype(BF16)
        kr = _head_block_rms(latk[:, KV_LORA:], MLA_ROPE, gkr_ref[...])
        kr = _rope_tok(kr, ck_ref[tok], sk_ref[tok], MLA_ROPE // 4)
        knp = jnp.dot(ckv, wuk_ref[...], preferred_element_type=F32)
        for j in range(MLA_HEADS):
            blk = _head_block_rms(knp[:, j * LANES:(j + 1) * LANES], MLA_NOPE, gkn_ref[...])
            k_ref[0, j, tok] = (blk + kr).astype(BF16)


def _prep_q(qT_ref, gq_ref, tab_ref, qn_ref, n, nope):
    nt, dq, tq = qT_ref.shape[2:]
    nf = tab_ref.shape[2]
    h, i = n // nt, n % nt
    q = qT_ref[0, h, i].astype(F32)
    g = gq_ref[...]
    parts = [_rms_feat(q[:nope], g[:nope])] if nope else []
    parts.append(_rope_feat(_rms_feat(q[nope:], g[nope:]), tab_ref[i], nf))
    parts.append(jnp.zeros((LANES - dq, tq), F32))
    qn_ref[...] = jnp.concatenate(parts, axis=0).astype(BF16)


def _store_out(oT_ref, vT_ref, p, n, heads_per_kv):
    nt = oT_ref.shape[1]
    dv = vT_ref.shape[2] - BF16_ROWS
    h, i = n // nt, n % nt
    o = jnp.dot(vT_ref[0, h // heads_per_kv], p, preferred_element_type=F32)
    out = (o[:dv] * (1.0 / o[dv:dv + 1])).astype(BF16)
    row0 = h * dv if isinstance(h, int) else pl.multiple_of(h * dv, dv)
    oT_ref[0, i, pl.ds(row0, dv), :] = out


def _attn_kernel(qT_ref, k_ref, vT_ref, gq_ref, tab_ref, oT_ref, qn_ref, s_ref, p_ref, *, nope):
    hb, nt = qT_ref.shape[1:3]
    heads_per_kv = hb // k_ref.shape[1]
    n_items = hb * nt

    def prep(n):
        _prep_q(qT_ref, gq_ref, tab_ref, qn_ref, n, nope)

    def scores(n):
        s = jnp.dot(k_ref[0, (n // nt) // heads_per_kv], qn_ref[...], preferred_element_type=F32)
        s_ref[...] = s
        return jnp.max(s, axis=0, keepdims=True)

    def probs(m):
        p_ref[...] = jnp.exp2(s_ref[...] - m).astype(BF16)

    def output(n):
        _store_out(oT_ref, vT_ref, p_ref[...], n, heads_per_kv)

    prep(0)
    m = scores(0)
    prep(1)
    probs(m)
    m = scores(1)
    prep(2)

    def body(n, m):
        output(n - 1)
        probs(m)
        m = scores(n + 1)
        prep(jnp.minimum(n + 2, n_items - 1))
        return m

    m = lax.fori_loop(1, n_items - 1, body, m)
    output(n_items - 2)
    probs(m)
    output(n_items - 1)


def _out_ffn_kernel(x_ref, oT_ref, wo_ref, gf_ref, wg_ref, wu_ref, wd_ref, out_ref, act_ref):
    y = [lax.dot_general(oT_ref[0, c], wo_ref[...], _TN, preferred_element_type=F32) for c in range(oT_ref.shape[1])]
    x1 = x_ref[0] + (y[0] if len(y) == 1 else jnp.concatenate(y, axis=0))
    h = _rms_tok(x1, gf_ref[...]).astype(BF16)
    for c in range(act_ref.shape[1] // FF_CHUNK):
        sl = slice(c * FF_CHUNK, (c + 1) * FF_CHUNK)
        g = jnp.dot(h, wg_ref[:, sl], preferred_element_type=F32)
        u = jnp.dot(h, wu_ref[:, sl], preferred_element_type=F32)
        act_ref[:, sl] = (g / (1.0 + jnp.exp(-g)) * u).astype(BF16)
    out_ref[0] = x1 + jnp.dot(act_ref[...], wd_ref[...], preferred_element_type=F32)


def _resident(shape):
    return pl.BlockSpec(shape, lambda *_: (0,) * len(shape), pipeline_mode=pl.Buffered(1))


def _params(vmem_mib, ndims):
    return pltpu.CompilerParams(dimension_semantics=("parallel",) * ndims, vmem_limit_bytes=vmem_mib * MIB)


def _rope_tables(s, rot_dim, lane_off):
    half = rot_dim // 2
    nf = half // 2
    tok = jnp.arange(s, dtype=jnp.int32)
    row = (tok // GRID_W).astype(F32)
    col = (tok % GRID_W).astype(F32)
    inv = ROPE_THETA ** (-jnp.arange(0, half, 2, dtype=F32) / half)
    ar = row[:, None] * inv[None, :]
    ac = col[:, None] * inv[None, :]
    cr, sr, cc, sc = jnp.cos(ar), jnp.sin(ar), jnp.cos(ac), jnp.sin(ac)
    tabq = jnp.stack([cr.T, sr.T, cc.T, sc.T]).reshape(4, nf, s // Q_TILE, Q_TILE).transpose(2, 0, 1, 3)
    lo = jnp.zeros((s, lane_off), F32)
    hi = jnp.zeros((s, LANES - lane_off - rot_dim), F32)
    cos_k = jnp.concatenate([lo, cr, cr, cc, cc, hi], axis=1)
    sin_k = jnp.concatenate([lo, -sr, sr, -sc, sc, hi], axis=1)
    return tabq, cos_k, sin_k


def _pad_heads_cols(w, heads, d):
    w = w.reshape(w.shape[0], heads, d)
    return jnp.pad(w, ((0, 0), (0, 0), (0, LANES - d))).reshape(w.shape[0], heads * LANES)


def _proj_out(b, s, t, heads, kv_heads, dq, dv):
    specs = [
        pl.BlockSpec((1, heads, t // Q_TILE, dq, Q_TILE), lambda bi, i: (bi, 0, i, 0, 0)),
        pl.BlockSpec((1, kv_heads, t, LANES), lambda bi, i: (bi, 0, i, 0)),
        pl.BlockSpec((1, kv_heads, dv + BF16_ROWS, t), lambda bi, i: (bi, 0, 0, i)),
    ]
    shapes = [
        jax.ShapeDtypeStruct((b, heads, s // Q_TILE, dq, Q_TILE), BF16),
        jax.ShapeDtypeStruct((b, kv_heads, s, LANES), BF16),
        jax.ShapeDtypeStruct((b, kv_heads, dv + BF16_ROWS, s), BF16),
    ]
    return specs, shapes


def _gqa_proj(x, norm_g, w_qkv, k_norm, tabs):
    b, s, d = x.shape
    t = TOKEN_TILE
    nq = GQA_HEADS * GQA_HD
    nk = GQA_KV_HEADS * GQA_HD
    _, cos_k, sin_k = tabs
    wqT = w_qkv[:, :nq].T.astype(BF16)
    wk = _pad_heads_cols(w_qkv[:, nq:nq + nk], GQA_KV_HEADS, GQA_HD).astype(BF16)
    wvT = w_qkv[:, nq + nk:].T.astype(BF16)
    gk = jnp.pad(k_norm, (0, LANES - GQA_HD)).reshape(1, LANES)
    out_specs, out_shape = _proj_out(b, s, t, GQA_HEADS, GQA_KV_HEADS, GQA_HD, GQA_HD)
    return pl.pallas_call(
        _gqa_proj_kernel,
        grid=(b, s // t),
        in_specs=[
            pl.BlockSpec((1, t, d), lambda bi, i: (bi, i, 0)),
            _resident((1, d)),
            _resident((nq, d)),
            _resident((d, GQA_KV_HEADS * LANES)),
            _resident((nk, d)),
            _resident((1, LANES)),
            pl.BlockSpec((t, LANES), lambda bi, i: (i, 0)),
            pl.BlockSpec((t, LANES), lambda bi, i: (i, 0)),
        ],
        out_specs=out_specs,
        out_shape=out_shape,
        compiler_params=_params(40, 2),
        name="gqa_proj",
    )(x, norm_g.reshape(1, d), wqT, wk, wvT, gk, cos_k, sin_k)


def _mla_proj(x, norm_g, w_in, q_lora_norm, w_uq, kv_lora_norm, w_ukv, k_norm, tabs):
    b, s, d = x.shape
    t = TOKEN_TILE
    _, cos_k, sin_k = tabs
    nlat = Q_LORA + KV_LORA
    winT = w_in[:, :nlat].T.astype(BF16)
    w_kr = jnp.pad(w_in[:, nlat:], ((0, 0), (MLA_NOPE, LANES - MLA_QK)))
    wink = jnp.concatenate([w_in[:, Q_LORA:nlat], w_kr], axis=1).astype(BF16)
    wuqT = w_uq.T.astype(BF16)
    wukv = w_ukv.reshape(KV_LORA, MLA_HEADS, MLA_NOPE + MLA_V)
    wuk = _pad_heads_cols(wukv[:, :, :MLA_NOPE].reshape(KV_LORA, MLA_HEADS * MLA_NOPE), MLA_HEADS, MLA_NOPE).astype(BF16)
    wuvT = wukv[:, :, MLA_NOPE:].reshape(KV_LORA, MLA_HEADS * MLA_V).T.astype(BF16)
    gkn = jnp.pad(k_norm[:MLA_NOPE], (0, LANES - MLA_NOPE)).reshape(1, LANES)
    gkr = jnp.pad(k_norm[MLA_NOPE:], (MLA_NOPE, LANES - MLA_QK)).reshape(1, LANES)
    out_specs, out_shape = _proj_out(b, s, t, MLA_HEADS, MLA_HEADS, MLA_QK, MLA_V)
    return pl.pallas_call(
        _mla_proj_kernel,
        grid=(b, s // t),
        in_specs=[
            pl.BlockSpec((1, t, d), lambda bi, i: (bi, i, 0)),
            _resident((1, d)),
            _resident((nlat, d)),
            _resident((d, KV_LORA + LANES)),
            _resident((Q_LORA, 1)),
            _resident((KV_LORA, 1)),
            _resident((1, KV_LORA)),
            _resident((MLA_HEADS * MLA_QK, Q_LORA)),
            _resident((MLA_HEADS * MLA_V, KV_LORA)),
            _resident((KV_LORA, MLA_HEADS * LANES)),
            _resident((1, LANES)),
            _resident((1, LANES)),
            pl.BlockSpec((t, LANES), lambda bi, i: (i, 0)),
            pl.BlockSpec((t, LANES), lambda bi, i: (i, 0)),
        ],
        out_specs=out_specs,
        out_shape=out_shape,
        compiler_params=_params(48, 2),
        name="mla_proj",
    )(x, norm_g.reshape(1, d), winT, wink, q_lora_norm.reshape(Q_LORA, 1), kv_lora_norm.reshape(KV_LORA, 1),
      kv_lora_norm.reshape(1, KV_LORA), wuqT, wuvT, wuk, gkn, gkr, cos_k, sin_k)


def _attn_head_block(heads, kv_heads, nt, dq, tq, s, dva):
    hb = heads
    while hb > heads // kv_heads:
        nk = hb * kv_heads // heads
        block_bytes = 2 * (hb * nt * dq * tq + nk * s * LANES + nk * dva * s + nt * hb * (dva - BF16_ROWS) * tq)
        if 2 * block_bytes <= ATTN_BLOCK_BUDGET:
            break
        hb //= 2
    return hb


def _attention(qT, k, vT, q_gain, tabq, nope):
    b, heads, nt, dq, tq = qT.shape
    kv_heads, s = k.shape[1], k.shape[2]
    dva = vT.shape[2]
    dv = dva - BF16_ROWS
    hb = _attn_head_block(heads, kv_heads, nt, dq, tq, s, dva)
    nk = hb * kv_heads // heads
    nf = tabq.shape[2]
    return pl.pallas_call(
        functools.partial(_attn_kernel, nope=nope),
        grid=(b, heads // hb),
        in_specs=[
            pl.BlockSpec((1, hb, nt, dq, tq), lambda bi, g: (bi, g, 0, 0, 0)),
            pl.BlockSpec((1, nk, s, LANES), lambda bi, g: (bi, g, 0, 0)),
            pl.BlockSpec((1, nk, dva, s), lambda bi, g: (bi, g, 0, 0)),
            _resident((dq, 1)),
            _resident((nt, 4, nf, tq)),
        ],
        out_specs=pl.BlockSpec((1, nt, hb * dv, tq), lambda bi, g: (bi, 0, g, 0)),
        out_shape=jax.ShapeDtypeStruct((b, nt, heads * dv, tq), BF16),
        scratch_shapes=[pltpu.VMEM((LANES, tq), BF16), pltpu.VMEM((s, tq), F32), pltpu.VMEM((s, tq), BF16)],
        compiler_params=_params(48, 2),
        name="attention",
    )(qT, k, vT, q_gain.reshape(dq, 1), tabq)


def _out_ffn(x, oT, w_o, ffn_g, w_gate_up, w_down):
    b, s, d = x.shape
    t = FFN_TOKEN_TILE
    dff = w_down.shape[0]
    tq = oT.shape[3]
    wg = w_gate_up[:, :dff].astype(BF16)
    wu = w_gate_up[:, dff:].astype(BF16)
    return pl.pallas_call(
        _out_ffn_kernel,
        grid=(b, s // t),
        in_specs=[
            pl.BlockSpec((1, t, d), lambda bi, i: (bi, i, 0)),
            pl.BlockSpec((1, t // tq, oT.shape[2], tq), lambda bi, i: (bi, i, 0, 0)),
            _resident(w_o.shape),
            _resident((1, d)),
            _resident((d, dff)),
            _resident((d, dff)),
            _resident((dff, d)),
        ],
        out_specs=pl.BlockSpec((1, t, d), lambda bi, i: (bi, i, 0)),
        out_shape=jax.ShapeDtypeStruct((b, s, d), F32),
        scratch_shapes=[pltpu.VMEM((t, dff), BF16)],
        compiler_params=_params(60, 2),
        name="out_ffn",
    )(x, oT, w_o.astype(BF16), ffn_g.reshape(1, d), wg, wu, w_down.astype(BF16))


def kernel(x, mla_norm, mla_w_in, mla_q_lora_norm, mla_w_uq, mla_kv_lora_norm, mla_w_ukv, mla_q_norm, mla_k_norm, mla_w_o, gqa_norm, gqa_w_qkv, gqa_q_norm, gqa_k_norm, gqa_w_o, ffn_norm, ffn_w_gate_up, ffn_w_down):
    s = x.shape[1]
    mla_tabs = _rope_tables(s, MLA_ROPE, MLA_NOPE)
    gqa_tabs = _rope_tables(s, GQA_HD, 0)
    depth = ffn_norm.shape[0]
    for i in range(depth):
        j = i // 2
        if i % 2 == 0:
            qT, k, vT = _mla_proj(x, mla_norm[j], mla_w_in[j], mla_q_lora_norm[j], mla_w_uq[j],
                                  mla_kv_lora_norm[j], mla_w_ukv[j], mla_k_norm[j], mla_tabs)
            q_gain = mla_q_norm[j] * (MLA_QK ** -0.5 * LOG2E)
            oT = _attention(qT, k, vT, q_gain, mla_tabs[0], MLA_NOPE)
            w_o = mla_w_o[j]
        else:
            qT, k, vT = _gqa_proj(x, gqa_norm[j], gqa_w_qkv[j], gqa_k_norm[j], gqa_tabs)
            q_gain = gqa_q_norm[j] * (GQA_HD ** -0.5 * LOG2E)
            oT = _attention(qT, k, vT, q_gain, gqa_tabs[0], 0)
            w_o = gqa_w_o[j]
        x = _out_ffn(x, oT, w_o, ffn_norm[i], ffn_w_gate_up[i], ffn_w_down[i])
    return x
```

```python
import functools
import math

import jax
import jax.numpy as jnp
from jax import lax
from jax.experimental import pallas as pl
from jax.experimental.pallas import tpu as pltpu

GRID_W = 64
ROPE_THETA = 10000.0
EPS = 1e-6

MLA_HEADS = 16
MLA_NOPE = 64
MLA_ROPE = 32
MLA_QK = MLA_NOPE + MLA_ROPE
MLA_V = 64
Q_LORA = 384
KV_LORA = 256

GQA_HEADS = 16
GQA_KV_HEADS = 4
GQA_HD = 64

LANES = 128
BF16_ROWS = 16
TOKEN_TILE = 512
FFN_TOKEN_TILE = 1024
GQA_SUB_TILE = 256
MLA_SUB_TILE = 256
Q_TILE = 1024
Q_STORE_TILE = 512
FF_CHUNK = 256
MIB = 1024 * 1024
ATTN_BLOCK_BUDGET = 36 * MIB
LOG2E = math.log2(math.e)
MAX_CONST_SHIFT = 60.0
NORM_SLACK = 1.02

F32 = jnp.float32
BF16 = jnp.bfloat16

_NT = (((1,), (1,)), ((), ()))
_TN = (((0,), (0,)), ((), ()))


def _rms_tok(x, g):
    return x * lax.rsqrt(jnp.mean(x * x, axis=-1, keepdims=True) + EPS) * g


def _rms_feat(x, g):
    return x * lax.rsqrt(jnp.mean(x * x, axis=0, keepdims=True) + EPS) * g


def _rope_feat(x, tab, nf):
    cr, sr, cc, sc = tab[0], tab[1], tab[2], tab[3]
    a, b, c, d = (x[i * nf:(i + 1) * nf] for i in range(4))
    return jnp.concatenate([a * cr - b * sr, b * cr + a * sr, c * cc - d * sc, d * cc + c * sc], axis=0)


def _rope_tok(x, cos, sin_signed, nf):
    lane = lax.broadcasted_iota(jnp.int32, x.shape, 1)
    first = (lane % (2 * nf)) < nf
    partner = jnp.where(first, pltpu.roll(x, LANES - nf, 1), pltpu.roll(x, nf, 1))
    return x * cos + partner * sin_signed


def _head_block_rms(blk, real, g):
    ss = jnp.sum(blk * blk, axis=-1, keepdims=True)
    return blk * lax.rsqrt(ss * (1.0 / real) + EPS) * g


def _store_q(qT_ref, qT, heads, tok):
    q3 = qT.reshape(heads, qT.shape[0] // heads, qT.shape[1]).astype(BF16)
    c, off = tok.start // Q_STORE_TILE, tok.start % Q_STORE_TILE
    assert off + q3.shape[2] <= Q_STORE_TILE, "a token pass must stay inside one query tile"
    qT_ref[0, :, c, :, off:off + q3.shape[2]] = q3


def _store_v_with_ones(vT_ref, v3, tok):
    heads, _, n = v3.shape
    row = lax.broadcasted_iota(jnp.int32, (heads, BF16_ROWS, n), 1)
    ones_row = jnp.where(row == 0, 1.0, 0.0).astype(F32)
    vT_ref[0, :, :, tok] = jnp.concatenate([v3, ones_row], axis=1).astype(BF16)


def _sub_tiles(t, sub):
    return [slice(c * sub, (c + 1) * sub) for c in range(t // sub)]


def _gqa_proj_kernel(x_ref, g_ref, wqT_ref, wk_ref, wvT_ref, gk_ref, ck_ref, sk_ref, qT_ref, k_ref, vT_ref):
    for tok in _sub_tiles(x_ref.shape[1], GQA_SUB_TILE):
        n = tok.stop - tok.start
        h = _rms_tok(x_ref[0, tok], g_ref[...]).astype(BF16)
        qT = lax.dot_general(wqT_ref[...], h, _NT, preferred_element_type=F32)
        vT = lax.dot_general(wvT_ref[...], h, _NT, preferred_element_type=F32)
        kp = jnp.dot(h, wk_ref[...], preferred_element_type=F32)

        _store_q(qT_ref, qT, GQA_HEADS, tok)
        _store_v_with_ones(vT_ref, vT.reshape(GQA_KV_HEADS, GQA_HD, n), tok)

        cos, sin_signed = ck_ref[tok], sk_ref[tok]
        for j in range(GQA_KV_HEADS):
            blk = _head_block_rms(kp[:, j * LANES:(j + 1) * LANES], GQA_HD, gk_ref[...])
            k_ref[0, j, tok] = _rope_tok(blk, cos, sin_signed, GQA_HD // 4).astype(BF16)


def _mla_proj_kernel(x_ref, g_ref, winT_ref, wink_ref, gqlT_ref, gkvlT_ref, gkvl_ref, wuqT_ref, wuvT_ref,
                     wuk_ref, gkn_ref, gkr_ref, ck_ref, sk_ref, qT_ref, k_ref, vT_ref):
    t = x_ref.shape[1]
    h = _rms_tok(x_ref[0], g_ref[...]).astype(BF16)

    latT = lax.dot_general(winT_ref[...], h, _NT, preferred_element_type=F32)
    cqT = _rms_feat(latT[:Q_LORA], gqlT_ref[...]).astype(BF16)
    ckvT = _rms_feat(latT[Q_LORA:], gkvlT_ref[...]).astype(BF16)
    _store_q(qT_ref, jnp.dot(wuqT_ref[...], cqT, preferred_element_type=F32), MLA_HEADS, slice(0, t))
    vT = jnp.dot(wuvT_ref[...], ckvT, preferred_element_type=F32)
    _store_v_with_ones(vT_ref, vT.reshape(MLA_HEADS, MLA_V, t), slice(0, t))

    for tok in _sub_tiles(t, MLA_SUB_TILE):
        latk = jnp.dot(h[tok], wink_ref[...], preferred_element_type=F32)
        ckv = _rms_tok(latk[:, :KV_LORA], gkvl_ref[...]).astype(BF16)
        kr = _head_block_rms(latk[:, KV_LORA:], MLA_ROPE, gkr_ref[...])
        kr = _rope_tok(kr, ck_ref[tok], sk_ref[tok], MLA_ROPE // 4)
        knp = jnp.dot(ckv, wuk_ref[...], preferred_element_type=F32)
        for j in range(MLA_HEADS):
            blk = _head_block_rms(knp[:, j * LANES:(j + 1) * LANES], MLA_NOPE, gkn_ref[...])
            k_ref[0, j, tok] = (blk + kr).astype(BF16)


def _prep_q(qT_ref, gq_ref, tab_ref, qn_ref, n, nope):
    dq, tqs = qT_ref.shape[3:]
    nt, _, nf, tq = tab_ref.shape
    sub = tq // tqs
    h, i = n // nt, n % nt
    q = jnp.concatenate([qT_ref[0, h, sub * i + c] for c in range(sub)], axis=1).astype(F32)
    g = gq_ref[...]
    parts = [_rms_feat(q[:nope], g[:nope])] if nope else []
    parts.append(_rope_feat(_rms_feat(q[nope:], g[nope:]), tab_ref[i], nf))
    parts.append(jnp.zeros((LANES - dq, tq), F32))
    qn_ref[...] = jnp.concatenate(parts, axis=0).astype(BF16)


def _store_out(oT_ref, vT_ref, p, n, heads_per_kv):
    nt = oT_ref.shape[1]
    dv = vT_ref.shape[2] - BF16_ROWS
    h, i = n // nt, n % nt
    o = jnp.dot(vT_ref[0, h // heads_per_kv], p, preferred_element_type=F32)
    out = (o[:dv] * (1.0 / o[dv:dv + 1])).astype(BF16)
    row0 = h * dv if isinstance(h, int) else pl.multiple_of(h * dv, dv)
    oT_ref[0, i, pl.ds(row0, dv), :] = out


def _attn_kernel(qT_ref, k_ref, vT_ref, gq_ref, tab_ref, oT_ref, qn_ref, s_ref, p_ref, *, nope):
    hb, nt = qT_ref.shape[1], tab_ref.shape[0]
    heads_per_kv = hb // k_ref.shape[1]
    n_items = hb * nt

    def prep(n):
        _prep_q(qT_ref, gq_ref, tab_ref, qn_ref, n, nope)

    def scores(n):
        s = jnp.dot(k_ref[0, (n // nt) // heads_per_kv], qn_ref[...], preferred_element_type=F32)
        s_ref[...] = s
        return jnp.max(s, axis=0, keepdims=True)

    def probs(m):
        p_ref[...] = jnp.exp2(s_ref[...] - m).astype(BF16)

    def output(n):
        _store_out(oT_ref, vT_ref, p_ref[...], n, heads_per_kv)

    prep(0)
    m = scores(0)
    prep(1)
    probs(m)
    m = scores(1)
    prep(2)

    def body(n, m):
        output(n - 1)
        probs(m)
        m = scores(n + 1)
        prep(jnp.minimum(n + 2, n_items - 1))
        return m

    m = lax.fori_loop(1, n_items - 1, body, m)
    output(n_items - 2)
    probs(m)
    output(n_items - 1)


def _attn_const_shift_kernel(qT_ref, k_ref, vT_ref, gq_ref, tab_ref, shift_ref, oT_ref, qn_ref, p_ref, *, nope):
    hb, nt = qT_ref.shape[1], tab_ref.shape[0]
    heads_per_kv = hb // k_ref.shape[1]
    n_items = hb * nt
    shift = shift_ref[...]

    def prep(n):
        _prep_q(qT_ref, gq_ref, tab_ref, qn_ref, n, nope)

    def probs(n):
        s = jnp.dot(k_ref[0, (n // nt) // heads_per_kv], qn_ref[...], preferred_element_type=F32)
        p_ref[...] = jnp.exp2(s - shift).astype(BF16)

    def output(n):
        _store_out(oT_ref, vT_ref, p_ref[...], n, heads_per_kv)

    prep(0)
    probs(0)
    prep(1)

    def body(n, carry):
        output(n - 1)
        probs(n)
        prep(jnp.minimum(n + 1, n_items - 1))
        return carry

    lax.fori_loop(1, n_items, body, 0)
    output(n_items - 1)


def _out_ffn_kernel(x_ref, oT_ref, wo_ref, gf_ref, wg_ref, wu_ref, wd_ref, out_ref, act_ref):
    y = [lax.dot_general(oT_ref[0, c], wo_ref[...], _TN, preferred_element_type=F32) for c in range(oT_ref.shape[1])]
    x1 = x_ref[0] + (y[0] if len(y) == 1 else jnp.concatenate(y, axis=0))
    h = _rms_tok(x1, gf_ref[...]).astype(BF16)
    for c in range(act_ref.shape[1] // FF_CHUNK):
        sl = slice(c * FF_CHUNK, (c + 1) * FF_CHUNK)
        g = jnp.dot(h, wg_ref[:, sl], preferred_element_type=F32)
        u = jnp.dot(h, wu_ref[:, sl], preferred_element_type=F32)
        act_ref[:, sl] = (g / (1.0 + jnp.exp(-g)) * u).astype(BF16)
    out_ref[0] = x1 + jnp.dot(act_ref[...], wd_ref[...], preferred_element_type=F32)


def _resident(shape):
    return pl.BlockSpec(shape, lambda *_: (0,) * len(shape), pipeline_mode=pl.Buffered(1))


def _params(vmem_mib, ndims):
    return pltpu.CompilerParams(dimension_semantics=("parallel",) * ndims, vmem_limit_bytes=vmem_mib * MIB)


def _rope_tables(s, rot_dim, lane_off):
    half = rot_dim // 2
    nf = half // 2
    tok = jnp.arange(s, dtype=jnp.int32)
    row = (tok // GRID_W).astype(F32)
    col = (tok % GRID_W).astype(F32)
    inv = ROPE_THETA ** (-jnp.arange(0, half, 2, dtype=F32) / half)
    ar = row[:, None] * inv[None, :]
    ac = col[:, None] * inv[None, :]
    cr, sr, cc, sc = jnp.cos(ar), jnp.sin(ar), jnp.cos(ac), jnp.sin(ac)
    tabq = jnp.stack([cr.T, sr.T, cc.T, sc.T]).reshape(4, nf, s // Q_TILE, Q_TILE).transpose(2, 0, 1, 3)
    lo = jnp.zeros((s, lane_off), F32)
    hi = jnp.zeros((s, LANES - lane_off - rot_dim), F32)
    cos_k = jnp.concatenate([lo, cr, cr, cc, cc, hi], axis=1)
    sin_k = jnp.concatenate([lo, -sr, sr, -sc, sc, hi], axis=1)
    return tabq, cos_k, sin_k


def _pad_heads_cols(w, heads, d):
    w = w.reshape(w.shape[0], heads, d)
    return jnp.pad(w, ((0, 0), (0, 0), (0, LANES - d))).reshape(w.shape[0], heads * LANES)


def _proj_out(b, s, t, heads, kv_heads, dq, dv):
    specs = [
        pl.BlockSpec((1, heads, t // Q_STORE_TILE, dq, Q_STORE_TILE), lambda bi, i: (bi, 0, i, 0, 0)),
        pl.BlockSpec((1, kv_heads, t, LANES), lambda bi, i: (bi, 0, i, 0)),
        pl.BlockSpec((1, kv_heads, dv + BF16_ROWS, t), lambda bi, i: (bi, 0, 0, i)),
    ]
    shapes = [
        jax.ShapeDtypeStruct((b, heads, s // Q_STORE_TILE, dq, Q_STORE_TILE), BF16),
        jax.ShapeDtypeStruct((b, kv_heads, s, LANES), BF16),
        jax.ShapeDtypeStruct((b, kv_heads, dv + BF16_ROWS, s), BF16),
    ]
    return specs, shapes


def _gqa_proj(x, norm_g, w_qkv, k_norm, tabs):
    b, s, d = x.shape
    t = TOKEN_TILE
    nq = GQA_HEADS * GQA_HD
    nk = GQA_KV_HEADS * GQA_HD
    _, cos_k, sin_k = tabs
    wqT = w_qkv[:, :nq].T.astype(BF16)
    wk = _pad_heads_cols(w_qkv[:, nq:nq + nk], GQA_KV_HEADS, GQA_HD).astype(BF16)
    wvT = w_qkv[:, nq + nk:].T.astype(BF16)
    gk = jnp.pad(k_norm, (0, LANES - GQA_HD)).reshape(1, LANES)
    out_specs, out_shape = _proj_out(b, s, t, GQA_HEADS, GQA_KV_HEADS, GQA_HD, GQA_HD)
    return pl.pallas_call(
        _gqa_proj_kernel,
        grid=(b, s // t),
        in_specs=[
            pl.BlockSpec((1, t, d), lambda bi, i: (bi, i, 0)),
            _resident((1, d)),
            _resident((nq, d)),
            _resident((d, GQA_KV_HEADS * LANES)),
            _resident((nk, d)),
            _resident((1, LANES)),
            pl.BlockSpec((t, LANES), lambda bi, i: (i, 0)),
            pl.BlockSpec((t, LANES), lambda bi, i: (i, 0)),
        ],
        out_specs=out_specs,
        out_shape=out_shape,
        compiler_params=_params(40, 2),
        name="gqa_proj",
    )(x, norm_g.reshape(1, d), wqT, wk, wvT, gk, cos_k, sin_k)


def _mla_proj(x, norm_g, w_in, q_lora_norm, w_uq, kv_lora_norm, w_ukv, k_norm, tabs):
    b, s, d = x.shape
    t = TOKEN_TILE
    _, cos_k, sin_k = tabs
    nlat = Q_LORA + KV_LORA
    winT = w_in[:, :nlat].T.astype(BF16)
    w_kr = jnp.pad(w_in[:, nlat:], ((0, 0), (MLA_NOPE, LANES - MLA_QK)))
    wink = jnp.concatenate([w_in[:, Q_LORA:nlat], w_kr], axis=1).astype(BF16)
    wuqT = w_uq.T.astype(BF16)
    wukv = w_ukv.reshape(KV_LORA, MLA_HEADS, MLA_NOPE + MLA_V)
    wuk = _pad_heads_cols(wukv[:, :, :MLA_NOPE].reshape(KV_LORA, MLA_HEADS * MLA_NOPE), MLA_HEADS, MLA_NOPE).astype(BF16)
    wuvT = wukv[:, :, MLA_NOPE:].reshape(KV_LORA, MLA_HEADS * MLA_V).T.astype(BF16)
    gkn = jnp.pad(k_norm[:MLA_NOPE], (0, LANES - MLA_NOPE)).reshape(1, LANES)
    gkr = jnp.pad(k_norm[MLA_NOPE:], (MLA_NOPE, LANES - MLA_QK)).reshape(1, LANES)
    out_specs, out_shape = _proj_out(b, s, t, MLA_HEADS, MLA_HEADS, MLA_QK, MLA_V)
    return pl.pallas_call(
        _mla_proj_kernel,
        grid=(b, s // t),
        in_specs=[
            pl.BlockSpec((1, t, d), lambda bi, i: (bi, i, 0)),
            _resident((1, d)),
            _resident((nlat, d)),
            _resident((d, KV_LORA + LANES)),
            _resident((Q_LORA, 1)),
            _resident((KV_LORA, 1)),
            _resident((1, KV_LORA)),
            _resident((MLA_HEADS * MLA_QK, Q_LORA)),
            _resident((MLA_HEADS * MLA_V, KV_LORA)),
            _resident((KV_LORA, MLA_HEADS * LANES)),
            _resident((1, LANES)),
            _resident((1, LANES)),
            pl.BlockSpec((t, LANES), lambda bi, i: (i, 0)),
            pl.BlockSpec((t, LANES), lambda bi, i: (i, 0)),
        ],
        out_specs=out_specs,
        out_shape=out_shape,
        compiler_params=_params(48, 2),
        name="mla_proj",
    )(x, norm_g.reshape(1, d), winT, wink, q_lora_norm.reshape(Q_LORA, 1), kv_lora_norm.reshape(KV_LORA, 1),
      kv_lora_norm.reshape(1, KV_LORA), wuqT, wuvT, wuk, gkn, gkr, cos_k, sin_k)


def _attn_head_block(heads, kv_heads, nt, dq, tq, s, dva):
    hb = heads
    while hb > heads // kv_heads:
        nk = hb * kv_heads // heads
        block_bytes = 2 * (hb * nt * dq * tq + nk * s * LANES + nk * dva * s + nt * hb * (dva - BF16_ROWS) * tq)
        if 2 * block_bytes <= ATTN_BLOCK_BUDGET:
            break
        hb //= 2
    return hb


def _score_bound(q_gain, k_gain, nope):
    def norm_bound(g):
        parts = [g[:nope], g[nope:]] if nope else [g]
        return jnp.sqrt(sum(p.shape[0] * jnp.max(jnp.abs(p)) ** 2 for p in parts))
    return NORM_SLACK * NORM_SLACK * norm_bound(q_gain) * norm_bound(k_gain)


def _attention(qT, k, vT, q_gain, k_gain, tabq, nope):
    b, heads, nts, dq, tqs = qT.shape
    nt, _, nf, tq = tabq.shape
    kv_heads, s = k.shape[1], k.shape[2]
    dva = vT.shape[2]
    dv = dva - BF16_ROWS
    hb = _attn_head_block(heads, kv_heads, nt, dq, tq, s, dva)
    nk = hb * kv_heads // heads
    in_specs = [
        pl.BlockSpec((1, hb, nts, dq, tqs), lambda bi, g: (bi, g, 0, 0, 0)),
        pl.BlockSpec((1, nk, s, LANES), lambda bi, g: (bi, g, 0, 0)),
        pl.BlockSpec((1, nk, dva, s), lambda bi, g: (bi, g, 0, 0)),
        _resident((dq, 1)),
        _resident((nt, 4, nf, tq)),
    ]
    common = dict(
        grid=(b, heads // hb),
        out_specs=pl.BlockSpec((1, nt, hb * dv, tq), lambda bi, g: (bi, 0, g, 0)),
        out_shape=jax.ShapeDtypeStruct((b, nt, heads * dv, tq), BF16),
        compiler_params=_params(48, 2),
    )
    q_scratch, p_scratch = pltpu.VMEM((LANES, tq), BF16), pltpu.VMEM((s, tq), BF16)
    gq = q_gain.reshape(dq, 1)
    shift = _score_bound(q_gain, k_gain, nope)

    def const_shift(shift):
        return pl.pallas_call(
            functools.partial(_attn_const_shift_kernel, nope=nope),
            in_specs=in_specs + [_resident((1, 1))],
            scratch_shapes=[q_scratch, p_scratch],
            name="attention_const_shift", **common,
        )(qT, k, vT, gq, tabq, shift.reshape(1, 1))

    def exact_max(shift):
        del shift
        return pl.pallas_call(
            functools.partial(_attn_kernel, nope=nope),
            in_specs=in_specs,
            scratch_shapes=[q_scratch, pltpu.VMEM((s, tq), F32), p_scratch],
            name="attention_exact_max", **common,
        )(qT, k, vT, gq, tabq)

    return lax.cond(shift <= MAX_CONST_SHIFT, const_shift, exact_max, shift)


def _out_ffn(x, oT, w_o, ffn_g, w_gate_up, w_down):
    b, s, d = x.shape
    t = FFN_TOKEN_TILE
    dff = w_down.shape[0]
    tq = oT.shape[3]
    wg = w_gate_up[:, :dff].astype(BF16)
    wu = w_gate_up[:, dff:].astype(BF16)
    return pl.pallas_call(
        _out_ffn_kernel,
        grid=(b, s // t),
        in_specs=[
            pl.BlockSpec((1, t, d), lambda bi, i: (bi, i, 0)),
            pl.BlockSpec((1, t // tq, oT.shape[2], tq), lambda bi, i: (bi, i, 0, 0)),
            _resident(w_o.shape),
            _resident((1, d)),
            _resident((d, dff)),
            _resident((d, dff)),
            _resident((dff, d)),
        ],
        out_specs=pl.BlockSpec((1, t, d), lambda bi, i: (bi, i, 0)),
        out_shape=jax.ShapeDtypeStruct((b, s, d), F32),
        scratch_shapes=[pltpu.VMEM((t, dff), BF16)],
        compiler_params=_params(60, 2),
        name="out_ffn",
    )(x, oT, w_o.astype(BF16), ffn_g.reshape(1, d), wg, wu, w_down.astype(BF16))


def kernel(x, mla_norm, mla_w_in, mla_q_lora_norm, mla_w_uq, mla_kv_lora_norm, mla_w_ukv, mla_q_norm, mla_k_norm, mla_w_o, gqa_norm, gqa_w_qkv, gqa_q_norm, gqa_k_norm, gqa_w_o, ffn_norm, ffn_w_gate_up, ffn_w_down):
    s = x.shape[1]
    mla_tabs = _rope_tables(s, MLA_ROPE, MLA_NOPE)
    gqa_tabs = _rope_tables(s, GQA_HD, 0)
    depth = ffn_norm.shape[0]
    for i in range(depth):
        j = i // 2
        if i % 2 == 0:
            qT, k, vT = _mla_proj(x, mla_norm[j], mla_w_in[j], mla_q_lora_norm[j], mla_w_uq[j],
                                  mla_kv_lora_norm[j], mla_w_ukv[j], mla_k_norm[j], mla_tabs)
            q_gain = mla_q_norm[j] * (MLA_QK ** -0.5 * LOG2E)
            oT = _attention(qT, k, vT, q_gain, mla_k_norm[j], mla_tabs[0], MLA_NOPE)
            w_o = mla_w_o[j]
        else:
            qT, k, vT = _gqa_proj(x, gqa_norm[j], gqa_w_qkv[j], gqa_k_norm[j], gqa_tabs)
            q_gain = gqa_q_norm[j] * (GQA_HD ** -0.5 * LOG2E)
            oT = _attention(qT, k, vT, q_gain, gqa_k_norm[j], gqa_tabs[0], 0)
            w_o = gqa_w_o[j]
        x = _out_ffn(x, oT, w_o, ffn_norm[i], ffn_w_gate_up[i], ffn_w_down[i])
    return x
```

```python
import functools
import math

import jax
import jax.numpy as jnp
from jax import lax
from jax.experimental import pallas as pl
from jax.experimental.pallas import tpu as pltpu

GRID_W = 64
ROPE_THETA = 10000.0
EPS = 1e-6

MLA_HEADS = 16
MLA_NOPE = 64
MLA_ROPE = 32
MLA_QK = MLA_NOPE + MLA_ROPE
MLA_V = 64
Q_LORA = 384
KV_LORA = 256

GQA_HEADS = 16
GQA_KV_HEADS = 4
GQA_HD = 64

LANES = 128
BF16_ROWS = 16
TOKEN_TILE = 512
FFN_TOKEN_TILE = 1024
GQA_SUB_TILE = 256
MLA_SUB_TILE = 256
Q_TILE = 1024
Q_STORE_TILE = 512
FF_CHUNK = 256
MIB = 1024 * 1024
ATTN_BLOCK_BUDGET = 36 * MIB
LOG2E = math.log2(math.e)
MAX_CONST_SHIFT = 60.0
NORM_SLACK = 1.02

F32 = jnp.float32
BF16 = jnp.bfloat16

_NT = (((1,), (1,)), ((), ()))
_TN = (((0,), (0,)), ((), ()))


def _rms_tok(x, g):
    return x * lax.rsqrt(jnp.mean(x * x, axis=-1, keepdims=True) + EPS) * g


def _rms_feat(x, g):
    return x * lax.rsqrt(jnp.mean(x * x, axis=0, keepdims=True) + EPS) * g


def _rope_feat(x, tab, nf):
    cr, sr, cc, sc = tab[0], tab[1], tab[2], tab[3]
    a, b, c, d = (x[i * nf:(i + 1) * nf] for i in range(4))
    return jnp.concatenate([a * cr - b * sr, b * cr + a * sr, c * cc - d * sc, d * cc + c * sc], axis=0)


def _rope_tok(x, cos, sin_signed, nf):
    lane = lax.broadcasted_iota(jnp.int32, x.shape, 1)
    first = (lane % (2 * nf)) < nf
    partner = jnp.where(first, pltpu.roll(x, LANES - nf, 1), pltpu.roll(x, nf, 1))
    return x * cos + partner * sin_signed


def _head_block_rms(blk, real, g):
    ss = jnp.sum(blk * blk, axis=-1, keepdims=True)
    return blk * lax.rsqrt(ss * (1.0 / real) + EPS) * g


def _store_q(qT_ref, qT, heads, tok):
    q3 = qT.reshape(heads, qT.shape[0] // heads, qT.shape[1]).astype(BF16)
    c, off = tok.start // Q_STORE_TILE, tok.start % Q_STORE_TILE
    assert off + q3.shape[2] <= Q_STORE_TILE, "a token pass must stay inside one query tile"
    qT_ref[0, :, c, :, off:off + q3.shape[2]] = q3


def _store_v_with_ones(vT_ref, v3, tok):
    heads, _, n = v3.shape
    row = lax.broadcasted_iota(jnp.int32, (heads, BF16_ROWS, n), 1)
    ones_row = jnp.where(row == 0, 1.0, 0.0).astype(F32)
    vT_ref[0, :, :, tok] = jnp.concatenate([v3, ones_row], axis=1).astype(BF16)


def _sub_tiles(t, sub):
    return [slice(c * sub, (c + 1) * sub) for c in range(t // sub)]


def _gqa_proj_kernel(x_ref, g_ref, wqT_ref, wk_ref, wvT_ref, gk_ref, ck_ref, sk_ref, qT_ref, k_ref, vT_ref):
    for tok in _sub_tiles(x_ref.shape[1], GQA_SUB_TILE):
        n = tok.stop - tok.start
        h = _rms_tok(x_ref[0, tok], g_ref[...]).astype(BF16)
        qT = lax.dot_general(wqT_ref[...], h, _NT, preferred_element_type=F32)
        vT = lax.dot_general(wvT_ref[...], h, _NT, preferred_element_type=F32)
        kp = jnp.dot(h, wk_ref[...], preferred_element_type=F32)

        _store_q(qT_ref, qT, GQA_HEADS, tok)
        _store_v_with_ones(vT_ref, vT.reshape(GQA_KV_HEADS, GQA_HD, n), tok)

        cos, sin_signed = ck_ref[tok], sk_ref[tok]
        for j in range(GQA_KV_HEADS):
            blk = _head_block_rms(kp[:, j * LANES:(j + 1) * LANES], GQA_HD, gk_ref[...])
            k_ref[0, j, tok] = _rope_tok(blk, cos, sin_signed, GQA_HD // 4).astype(BF16)


def _mla_proj_kernel(x_ref, g_ref, winT_ref, wink_ref, gqlT_ref, gkvlT_ref, gkvl_ref, wuqT_ref, wuvT_ref,
                     wuk_ref, gkn_ref, gkr_ref, ck_ref, sk_ref, qT_ref, k_ref, vT_ref):
    t = x_ref.shape[1]
    h = _rms_tok(x_ref[0], g_ref[...]).astype(BF16)

    latT = lax.dot_general(winT_ref[...], h, _NT, preferred_element_type=F32)
    cqT = _rms_feat(latT[:Q_LORA], gqlT_ref[...]).astype(BF16)
    ckvT = _rms_feat(latT[Q_LORA:], gkvlT_ref[...]).astype(BF16)
    _store_q(qT_ref, jnp.dot(wuqT_ref[...], cqT, preferred_element_type=F32), MLA_HEADS, slice(0, t))
    vT = jnp.dot(wuvT_ref[...], ckvT, preferred_element_type=F32)
    _store_v_with_ones(vT_ref, vT.reshape(MLA_HEADS, MLA_V, t), slice(0, t))

    for tok in _sub_tiles(t, MLA_SUB_TILE):
        latk = jnp.dot(h[tok], wink_ref[...], preferred_element_type=F32)
        ckv = _rms_tok(latk[:, :KV_LORA], gkvl_ref[...]).astype(BF16)
        kr = _head_block_rms(latk[:, KV_LORA:], MLA_ROPE, gkr_ref[...])
        kr = _rope_tok(kr, ck_ref[tok], sk_ref[tok], MLA_ROPE // 4)
        knp = jnp.dot(ckv, wuk_ref[...], preferred_element_type=F32)
        for j in range(MLA_HEADS):
            blk = _head_block_rms(knp[:, j * LANES:(j + 1) * LANES], MLA_NOPE, gkn_ref[...])
            k_ref[0, j, tok] = (blk + kr).astype(BF16)


def _prep_q(qT_ref, gq_ref, tab_ref, qn_ref, n, nope):
    dq, tqs = qT_ref.shape[3:]
    nt, _, nf, tq = tab_ref.shape
    sub = tq // tqs
    h, i = n // nt, n % nt
    q = jnp.concatenate([qT_ref[0, h, sub * i + c] for c in range(sub)], axis=1).astype(F32)
    g = gq_ref[...]
    parts = [_rms_feat(q[:nope], g[:nope])] if nope else []
    parts.append(_rope_feat(_rms_feat(q[nope:], g[nope:]), tab_ref[i], nf))
    parts.append(jnp.zeros((LANES - dq, tq), F32))
    qn_ref[...] = jnp.concatenate(parts, axis=0).astype(BF16)


def _store_out(oT_ref, vT_ref, p, n, heads_per_kv):
    nt = oT_ref.shape[1]
    dv = vT_ref.shape[2] - BF16_ROWS
    h, i = n // nt, n % nt
    o = jnp.dot(vT_ref[0, h // heads_per_kv], p, preferred_element_type=F32)
    out = (o[:dv] * (1.0 / o[dv:dv + 1])).astype(BF16)
    row0 = h * dv if isinstance(h, int) else pl.multiple_of(h * dv, dv)
    oT_ref[0, i, pl.ds(row0, dv), :] = out


def _attn_kernel(qT_ref, k_ref, vT_ref, gq_ref, tab_ref, oT_ref, qn_ref, s_ref, p_ref, *, nope):
    hb, nt = qT_ref.shape[1], tab_ref.shape[0]
    heads_per_kv = hb // k_ref.shape[1]
    n_items = hb * nt

    def prep(n):
        _prep_q(qT_ref, gq_ref, tab_ref, qn_ref, n, nope)

    def scores(n):
        s = jnp.dot(k_ref[0, (n // nt) // heads_per_kv], qn_ref[...], preferred_element_type=F32)
        s_ref[...] = s
        return jnp.max(s, axis=0, keepdims=True)

    def probs(m):
        p_ref[...] = jnp.exp2(s_ref[...] - m).astype(BF16)

    def output(n):
        _store_out(oT_ref, vT_ref, p_ref[...], n, heads_per_kv)

    prep(0)
    m = scores(0)
    prep(1)
    probs(m)
    m = scores(1)
    prep(2)

    def body(n, m):
        output(n - 1)
        probs(m)
        m = scores(n + 1)
        prep(jnp.minimum(n + 2, n_items - 1))
        return m

    m = lax.fori_loop(1, n_items - 1, body, m)
    output(n_items - 2)
    probs(m)
    output(n_items - 1)


def _attn_const_shift_kernel(qT_ref, k_ref, vT_ref, gq_ref, tab_ref, shift_ref, oT_ref, qn_ref, p_ref, *, nope):
    hb, nt = qT_ref.shape[1], tab_ref.shape[0]
    heads_per_kv = hb // k_ref.shape[1]
    n_items = hb * nt
    shift = shift_ref[...]

    def prep(n):
        _prep_q(qT_ref, gq_ref, tab_ref, qn_ref, n, nope)

    def probs(n):
        s = jnp.dot(k_ref[0, (n // nt) // heads_per_kv], qn_ref[...], preferred_element_type=F32)
        p_ref[...] = jnp.exp2(s - shift).astype(BF16)

    def output(n):
        _store_out(oT_ref, vT_ref, p_ref[...], n, heads_per_kv)

    prep(0)
    probs(0)
    prep(1)

    def body(n, carry):
        output(n - 1)
        probs(n)
        prep(jnp.minimum(n + 1, n_items - 1))
        return carry

    lax.fori_loop(1, n_items, body, 0)
    output(n_items - 1)


def _out_ffn_kernel(x_ref, oT_ref, wo_ref, gf_ref, wg_ref, wu_ref, wd_ref, out_ref, act_ref):
    nc, _, tq = oT_ref.shape[1:]
    cols = min(tq, Q_STORE_TILE)
    y = [lax.dot_general(oT_ref[0, c, :, j * cols:(j + 1) * cols], wo_ref[...], _TN, preferred_element_type=F32)
         for c in range(nc) for j in range(tq // cols)]
    x1 = x_ref[0] + (y[0] if len(y) == 1 else jnp.concatenate(y, axis=0))
    h = _rms_tok(x1, gf_ref[...]).astype(BF16)
    for c in range(act_ref.shape[1] // FF_CHUNK):
        sl = slice(c * FF_CHUNK, (c + 1) * FF_CHUNK)
        g = jnp.dot(h, wg_ref[:, sl], preferred_element_type=F32)
        u = jnp.dot(h, wu_ref[:, sl], preferred_element_type=F32)
        act_ref[:, sl] = (g / (1.0 + jnp.exp(-g)) * u).astype(BF16)
    out_ref[0] = x1 + jnp.dot(act_ref[...], wd_ref[...], preferred_element_type=F32)


def _resident(shape):
    return pl.BlockSpec(shape, lambda *_: (0,) * len(shape), pipeline_mode=pl.Buffered(1))


def _params(vmem_mib, ndims):
    return pltpu.CompilerParams(dimension_semantics=("parallel",) * ndims, vmem_limit_bytes=vmem_mib * MIB)


def _rope_tables(s, rot_dim, lane_off):
    half = rot_dim // 2
    nf = half // 2
    tok = jnp.arange(s, dtype=jnp.int32)
    row = (tok // GRID_W).astype(F32)
    col = (tok % GRID_W).astype(F32)
    inv = ROPE_THETA ** (-jnp.arange(0, half, 2, dtype=F32) / half)
    ar = row[:, None] * inv[None, :]
    ac = col[:, None] * inv[None, :]
    cr, sr, cc, sc = jnp.cos(ar), jnp.sin(ar), jnp.cos(ac), jnp.sin(ac)
    tabq = jnp.stack([cr.T, sr.T, cc.T, sc.T]).reshape(4, nf, s // Q_TILE, Q_TILE).transpose(2, 0, 1, 3)
    lo = jnp.zeros((s, lane_off), F32)
    hi = jnp.zeros((s, LANES - lane_off - rot_dim), F32)
    cos_k = jnp.concatenate([lo, cr, cr, cc, cc, hi], axis=1)
    sin_k = jnp.concatenate([lo, -sr, sr, -sc, sc, hi], axis=1)
    return tabq, cos_k, sin_k


def _pad_heads_cols(w, heads, d):
    w = w.reshape(w.shape[0], heads, d)
    return jnp.pad(w, ((0, 0), (0, 0), (0, LANES - d))).reshape(w.shape[0], heads * LANES)


def _proj_out(b, s, t, heads, kv_heads, dq, dv):
    specs = [
        pl.BlockSpec((1, heads, t // Q_STORE_TILE, dq, Q_STORE_TILE), lambda bi, i: (bi, 0, i, 0, 0)),
        pl.BlockSpec((1, kv_heads, t, LANES), lambda bi, i: (bi, 0, i, 0)),
        pl.BlockSpec((1, kv_heads, dv + BF16_ROWS, t), lambda bi, i: (bi, 0, 0, i)),
    ]
    shapes = [
        jax.ShapeDtypeStruct((b, heads, s // Q_STORE_TILE, dq, Q_STORE_TILE), BF16),
        jax.ShapeDtypeStruct((b, kv_heads, s, LANES), BF16),
        jax.ShapeDtypeStruct((b, kv_heads, dv + BF16_ROWS, s), BF16),
    ]
    return specs, shapes


def _gqa_proj(x, norm_g, w_qkv, k_norm, tabs):
    b, s, d = x.shape
    t = TOKEN_TILE
    nq = GQA_HEADS * GQA_HD
    nk = GQA_KV_HEADS * GQA_HD
    _, cos_k, sin_k = tabs
    wqT = w_qkv[:, :nq].T.astype(BF16)
    wk = _pad_heads_cols(w_qkv[:, nq:nq + nk], GQA_KV_HEADS, GQA_HD).astype(BF16)
    wvT = w_qkv[:, nq + nk:].T.astype(BF16)
    gk = jnp.pad(k_norm, (0, LANES - GQA_HD)).reshape(1, LANES)
    out_specs, out_shape = _proj_out(b, s, t, GQA_HEADS, GQA_KV_HEADS, GQA_HD, GQA_HD)
    return pl.pallas_call(
        _gqa_proj_kernel,
        grid=(b, s // t),
        in_specs=[
            pl.BlockSpec((1, t, d), lambda bi, i: (bi, i, 0)),
            _resident((1, d)),
            _resident((nq, d)),
            _resident((d, GQA_KV_HEADS * LANES)),
            _resident((nk, d)),
            _resident((1, LANES)),
            pl.BlockSpec((t, LANES), lambda bi, i: (i, 0)),
            pl.BlockSpec((t, LANES), lambda bi, i: (i, 0)),
        ],
        out_specs=out_specs,
        out_shape=out_shape,
        compiler_params=_params(40, 2),
        name="gqa_proj",
    )(x, norm_g.reshape(1, d), wqT, wk, wvT, gk, cos_k, sin_k)


def _mla_proj(x, norm_g, w_in, q_lora_norm, w_uq, kv_lora_norm, w_ukv, k_norm, tabs):
    b, s, d = x.shape
    t = TOKEN_TILE
    _, cos_k, sin_k = tabs
    nlat = Q_LORA + KV_LORA
    winT = w_in[:, :nlat].T.astype(BF16)
    w_kr = jnp.pad(w_in[:, nlat:], ((0, 0), (MLA_NOPE, LANES - MLA_QK)))
    wink = jnp.concatenate([w_in[:, Q_LORA:nlat], w_kr], axis=1).astype(BF16)
    wuqT = w_uq.T.astype(BF16)
    wukv = w_ukv.reshape(KV_LORA, MLA_HEADS, MLA_NOPE + MLA_V)
    wuk = _pad_heads_cols(wukv[:, :, :MLA_NOPE].reshape(KV_LORA, MLA_HEADS * MLA_NOPE), MLA_HEADS, MLA_NOPE).astype(BF16)
    wuvT = wukv[:, :, MLA_NOPE:].reshape(KV_LORA, MLA_HEADS * MLA_V).T.astype(BF16)
    gkn = jnp.pad(k_norm[:MLA_NOPE], (0, LANES - MLA_NOPE)).reshape(1, LANES)
    gkr = jnp.pad(k_norm[MLA_NOPE:], (MLA_NOPE, LANES - MLA_QK)).reshape(1, LANES)
    out_specs, out_shape = _proj_out(b, s, t, MLA_HEADS, MLA_HEADS, MLA_QK, MLA_V)
    return pl.pallas_call(
        _mla_proj_kernel,
        grid=(b, s // t),
        in_specs=[
            pl.BlockSpec((1, t, d), lambda bi, i: (bi, i, 0)),
            _resident((1, d)),
            _resident((nlat, d)),
            _resident((d, KV_LORA + LANES)),
            _resident((Q_LORA, 1)),
            _resident((KV_LORA, 1)),
            _resident((1, KV_LORA)),
            _resident((MLA_HEADS * MLA_QK, Q_LORA)),
            _resident((MLA_HEADS * MLA_V, KV_LORA)),
            _resident((KV_LORA, MLA_HEADS * LANES)),
            _resident((1, LANES)),
            _resident((1, LANES)),
            pl.BlockSpec((t, LANES), lambda bi, i: (i, 0)),
            pl.BlockSpec((t, LANES), lambda bi, i: (i, 0)),
        ],
        out_specs=out_specs,
        out_shape=out_shape,
        compiler_params=_params(48, 2),
        name="mla_proj",
    )(x, norm_g.reshape(1, d), winT, wink, q_lora_norm.reshape(Q_LORA, 1), kv_lora_norm.reshape(KV_LORA, 1),
      kv_lora_norm.reshape(1, KV_LORA), wuqT, wuvT, wuk, gkn, gkr, cos_k, sin_k)


def _attn_head_block(heads, kv_heads, nt, dq, tq, s, dva):
    hb = heads
    while hb > heads // kv_heads:
        nk = hb * kv_heads // heads
        block_bytes = 2 * (hb * nt * dq * tq + nk * s * LANES + nk * dva * s + nt * hb * (dva - BF16_ROWS) * tq)
        if 2 * block_bytes <= ATTN_BLOCK_BUDGET:
            break
        hb //= 2
    return hb


def _score_bound(q_gain, k_gain, nope):
    def norm_bound(g):
        parts = [g[:nope], g[nope:]] if nope else [g]
        return jnp.sqrt(sum(p.shape[0] * jnp.max(jnp.abs(p)) ** 2 for p in parts))
    return NORM_SLACK * NORM_SLACK * norm_bound(q_gain) * norm_bound(k_gain)


def _attention(qT, k, vT, q_gain, k_gain, tabq, nope):
    b, heads, nts, dq, tqs = qT.shape
    nt, _, nf, tq = tabq.shape
    kv_heads, s = k.shape[1], k.shape[2]
    dva = vT.shape[2]
    dv = dva - BF16_ROWS
    hb = _attn_head_block(heads, kv_heads, nt, dq, tq, s, dva)
    nk = hb * kv_heads // heads
    in_specs = [
        pl.BlockSpec((1, hb, nts, dq, tqs), lambda bi, g: (bi, g, 0, 0, 0)),
        pl.BlockSpec((1, nk, s, LANES), lambda bi, g: (bi, g, 0, 0)),
        pl.BlockSpec((1, nk, dva, s), lambda bi, g: (bi, g, 0, 0)),
        _resident((dq, 1)),
        _resident((nt, 4, nf, tq)),
    ]
    common = dict(
        grid=(b, heads // hb),
        out_specs=pl.BlockSpec((1, nt, hb * dv, tq), lambda bi, g: (bi, 0, g, 0)),
        out_shape=jax.ShapeDtypeStruct((b, nt, heads * dv, tq), BF16),
        compiler_params=_params(48, 2),
    )
    q_scratch, p_scratch = pltpu.VMEM((LANES, tq), BF16), pltpu.VMEM((s, tq), BF16)
    gq = q_gain.reshape(dq, 1)
    shift = _score_bound(q_gain, k_gain, nope)

    def const_shift(shift):
        return pl.pallas_call(
            functools.partial(_attn_const_shift_kernel, nope=nope),
            in_specs=in_specs + [_resident((1, 1))],
            scratch_shapes=[q_scratch, p_scratch],
            name="attention_const_shift", **common,
        )(qT, k, vT, gq, tabq, shift.reshape(1, 1))

    def exact_max(shift):
        del shift
        return pl.pallas_call(
            functools.partial(_attn_kernel, nope=nope),
            in_specs=in_specs,
            scratch_shapes=[q_scratch, pltpu.VMEM((s, tq), F32), p_scratch],
            name="attention_exact_max", **common,
        )(qT, k, vT, gq, tabq)

    return lax.cond(shift <= MAX_CONST_SHIFT, const_shift, exact_max, shift)


def _out_ffn(x, oT, w_o, ffn_g, w_gate_up, w_down):
    b, s, d = x.shape
    t = FFN_TOKEN_TILE
    dff = w_down.shape[0]
    tq = oT.shape[3]
    wg = w_gate_up[:, :dff].astype(BF16)
    wu = w_gate_up[:, dff:].astype(BF16)
    return pl.pallas_call(
        _out_ffn_kernel,
        grid=(b, s // t),
        in_specs=[
            pl.BlockSpec((1, t, d), lambda bi, i: (bi, i, 0)),
            pl.BlockSpec((1, t // tq, oT.shape[2], tq), lambda bi, i: (bi, i, 0, 0)),
            _resident(w_o.shape),
            _resident((1, d)),
            _resident((d, dff)),
            _resident((d, dff)),
            _resident((dff, d)),
        ],
        out_specs=pl.BlockSpec((1, t, d), lambda bi, i: (bi, i, 0)),
        out_shape=jax.ShapeDtypeStruct((b, s, d), F32),
        scratch_shapes=[pltpu.VMEM((t, dff), BF16)],
        compiler_params=_params(60, 2),
        name="out_ffn",
    )(x, oT, w_o.astype(BF16), ffn_g.reshape(1, d), wg, wu, w_down.astype(BF16))


def kernel(x, mla_norm, mla_w_in, mla_q_lora_norm, mla_w_uq, mla_kv_lora_norm, mla_w_ukv, mla_q_norm, mla_k_norm, mla_w_o, gqa_norm, gqa_w_qkv, gqa_q_norm, gqa_k_norm, gqa_w_o, ffn_norm, ffn_w_gate_up, ffn_w_down):
    s = x.shape[1]
    mla_tabs = _rope_tables(s, MLA_ROPE, MLA_NOPE)
    gqa_tabs = _rope_tables(s, GQA_HD, 0)
    depth = ffn_norm.shape[0]
    for i in range(depth):
        j = i // 2
        if i % 2 == 0:
            qT, k, vT = _mla_proj(x, mla_norm[j], mla_w_in[j], mla_q_lora_norm[j], mla_w_uq[j],
                                  mla_kv_lora_norm[j], mla_w_ukv[j], mla_k_norm[j], mla_tabs)
            q_gain = mla_q_norm[j] * (MLA_QK ** -0.5 * LOG2E)
            oT = _attention(qT, k, vT, q_gain, mla_k_norm[j], mla_tabs[0], MLA_NOPE)
            w_o = mla_w_o[j]
        else:
            qT, k, vT = _gqa_proj(x, gqa_norm[j], gqa_w_qkv[j], gqa_k_norm[j], gqa_tabs)
            q_gain = gqa_q_norm[j] * (GQA_HD ** -0.5 * LOG2E)
            oT = _attention(qT, k, vT, q_gain, gqa_k_norm[j], gqa_tabs[0], 0)
            w_o = gqa_w_o[j]
        x = _out_ffn(x, oT, w_o, ffn_norm[i], ffn_w_gate_up[i], ffn_w_down[i])
    return x
```

```python
import functools
import math

import jax
import jax.numpy as jnp
from jax import lax
from jax.experimental import pallas as pl
from jax.experimental.pallas import tpu as pltpu

GRID_W = 64
ROPE_THETA = 10000.0
EPS = 1e-6

MLA_HEADS = 16
MLA_NOPE = 64
MLA_ROPE = 32
MLA_QK = MLA_NOPE + MLA_ROPE
MLA_V = 64
Q_LORA = 384
KV_LORA = 256

GQA_HEADS = 16
GQA_KV_HEADS = 4
GQA_HD = 64

LANES = 128
BF16_ROWS = 16
TOKEN_TILE = 512
FFN_TOKEN_TILE = 1024
GQA_SUB_TILE = 256
MLA_SUB_TILE = 256
Q_TILE = 2048
Q_STORE_TILE = 512
FF_CHUNK = 256
MIB = 1024 * 1024
ATTN_VMEM_BUDGET = 42 * MIB
LOG2E = math.log2(math.e)
MAX_CONST_SHIFT = 60.0
NORM_SLACK = 1.02

F32 = jnp.float32
BF16 = jnp.bfloat16

_NT = (((1,), (1,)), ((), ()))
_TN = (((0,), (0,)), ((), ()))


def _rms_tok(x, g):
    return x * lax.rsqrt(jnp.mean(x * x, axis=-1, keepdims=True) + EPS) * g


def _rms_feat(x, g):
    return x * lax.rsqrt(jnp.mean(x * x, axis=0, keepdims=True) + EPS) * g


def _rope_feat(x, tab, nf):
    cr, sr, cc, sc = tab[0], tab[1], tab[2], tab[3]
    a, b, c, d = (x[i * nf:(i + 1) * nf] for i in range(4))
    return jnp.concatenate([a * cr - b * sr, b * cr + a * sr, c * cc - d * sc, d * cc + c * sc], axis=0)


def _rope_tok(x, cos, sin_signed, nf):
    lane = lax.broadcasted_iota(jnp.int32, x.shape, 1)
    first = (lane % (2 * nf)) < nf
    partner = jnp.where(first, pltpu.roll(x, LANES - nf, 1), pltpu.roll(x, nf, 1))
    return x * cos + partner * sin_signed


def _head_block_rms(blk, real, g):
    ss = jnp.sum(blk * blk, axis=-1, keepdims=True)
    return blk * lax.rsqrt(ss * (1.0 / real) + EPS) * g


def _store_q(qT_ref, qT, heads, tok):
    q3 = qT.reshape(heads, qT.shape[0] // heads, qT.shape[1]).astype(BF16)
    c, off = tok.start // Q_STORE_TILE, tok.start % Q_STORE_TILE
    assert off + q3.shape[2] <= Q_STORE_TILE, "a token pass must stay inside one query tile"
    qT_ref[0, :, c, :, off:off + q3.shape[2]] = q3


def _store_v_with_ones(vT_ref, v3, tok):
    heads, _, n = v3.shape
    row = lax.broadcasted_iota(jnp.int32, (heads, BF16_ROWS, n), 1)
    ones_row = jnp.where(row == 0, 1.0, 0.0).astype(F32)
    vT_ref[0, :, :, tok] = jnp.concatenate([v3, ones_row], axis=1).astype(BF16)


def _sub_tiles(t, sub):
    return [slice(c * sub, (c + 1) * sub) for c in range(t // sub)]


def _gqa_proj_kernel(x_ref, g_ref, wqT_ref, wk_ref, wvT_ref, gk_ref, ck_ref, sk_ref, qT_ref, k_ref, vT_ref):
    for tok in _sub_tiles(x_ref.shape[1], GQA_SUB_TILE):
        n = tok.stop - tok.start
        h = _rms_tok(x_ref[0, tok], g_ref[...]).astype(BF16)
        qT = lax.dot_general(wqT_ref[...], h, _NT, preferred_element_type=F32)
        vT = lax.dot_general(wvT_ref[...], h, _NT, preferred_element_type=F32)
        kp = jnp.dot(h, wk_ref[...], preferred_element_type=F32)

        _store_q(qT_ref, qT, GQA_HEADS, tok)
        _store_v_with_ones(vT_ref, vT.reshape(GQA_KV_HEADS, GQA_HD, n), tok)

        cos, sin_signed = ck_ref[tok], sk_ref[tok]
        for j in range(GQA_KV_HEADS):
            blk = _head_block_rms(kp[:, j * LANES:(j + 1) * LANES], GQA_HD, gk_ref[...])
            k_ref[0, j, tok] = _rope_tok(blk, cos, sin_signed, GQA_HD // 4).astype(BF16)


def _mla_proj_kernel(x_ref, g_ref, winT_ref, wink_ref, gqlT_ref, gkvlT_ref, gkvl_ref, wuqT_ref, wuvT_ref,
                     wuk_ref, gkn_ref, gkr_ref, ck_ref, sk_ref, qT_ref, k_ref, vT_ref):
    t = x_ref.shape[1]
    h = _rms_tok(x_ref[0], g_ref[...]).astype(BF16)

    latT = lax.dot_general(winT_ref[...], h, _NT, preferred_element_type=F32)
    cqT = _rms_feat(latT[:Q_LORA], gqlT_ref[...]).astype(BF16)
    ckvT = _rms_feat(latT[Q_LORA:], gkvlT_ref[...]).astype(BF16)
    _store_q(qT_ref, jnp.dot(wuqT_ref[...], cqT, preferred_element_type=F32), MLA_HEADS, slice(0, t))
    vT = jnp.dot(wuvT_ref[...], ckvT, preferred_element_type=F32)
    _store_v_with_ones(vT_ref, vT.reshape(MLA_HEADS, MLA_V, t), slice(0, t))

    for tok in _sub_tiles(t, MLA_SUB_TILE):
        latk = jnp.dot(h[tok], wink_ref[...], preferred_element_type=F32)
        ckv = _rms_tok(latk[:, :KV_LORA], gkvl_ref[...]).astype(BF16)
        kr = _head_block_rms(latk[:, KV_LORA:], MLA_ROPE, gkr_ref[...])
        kr = _rope_tok(kr, ck_ref[tok], sk_ref[tok], MLA_ROPE // 4)
        knp = jnp.dot(ckv, wuk_ref[...], preferred_element_type=F32)
        for j in range(MLA_HEADS):
            blk = _head_block_rms(knp[:, j * LANES:(j + 1) * LANES], MLA_NOPE, gkn_ref[...])
            k_ref[0, j, tok] = (blk + kr).astype(BF16)


def _prep_q(qT_ref, gq_ref, tab_ref, qn_ref, n, nope):
    dq, tqs = qT_ref.shape[3:]
    nt, _, nf, tq = tab_ref.shape
    sub = tq // tqs
    h, i = n // nt, n % nt
    q = jnp.concatenate([qT_ref[0, h, sub * i + c] for c in range(sub)], axis=1).astype(F32)
    g = gq_ref[...]
    parts = [_rms_feat(q[:nope], g[:nope])] if nope else []
    parts.append(_rope_feat(_rms_feat(q[nope:], g[nope:]), tab_ref[i], nf))
    parts.append(jnp.zeros((LANES - dq, tq), F32))
    qn_ref[...] = jnp.concatenate(parts, axis=0).astype(BF16)


def _store_out(oT_ref, vT_ref, p, n, heads_per_kv):
    nt = oT_ref.shape[1]
    dv = vT_ref.shape[2] - BF16_ROWS
    h, i = n // nt, n % nt
    o = jnp.dot(vT_ref[0, h // heads_per_kv], p, preferred_element_type=F32)
    out = (o[:dv] * (1.0 / o[dv:dv + 1])).astype(BF16)
    row0 = h * dv if isinstance(h, int) else pl.multiple_of(h * dv, dv)
    oT_ref[0, i, pl.ds(row0, dv), :] = out


def _attn_kernel(qT_ref, k_ref, vT_ref, gq_ref, tab_ref, oT_ref, qn_ref, s_ref, p_ref, *, nope):
    hb, nt = qT_ref.shape[1], tab_ref.shape[0]
    heads_per_kv = hb // k_ref.shape[1]
    n_items = hb * nt

    def prep(n):
        _prep_q(qT_ref, gq_ref, tab_ref, qn_ref, n, nope)

    def scores(n):
        s = jnp.dot(k_ref[0, (n // nt) // heads_per_kv], qn_ref[...], preferred_element_type=F32)
        s_ref[...] = s
        return jnp.max(s, axis=0, keepdims=True)

    def probs(m):
        p_ref[...] = jnp.exp2(s_ref[...] - m).astype(BF16)

    def output(n):
        _store_out(oT_ref, vT_ref, p_ref[...], n, heads_per_kv)

    prep(0)
    m = scores(0)
    prep(1)
    probs(m)
    m = scores(1)
    prep(2)

    def body(n, m):
        output(n - 1)
        probs(m)
        m = scores(n + 1)
        prep(jnp.minimum(n + 2, n_items - 1))
        return m

    m = lax.fori_loop(1, n_items - 1, body, m)
    output(n_items - 2)
    probs(m)
    output(n_items - 1)


def _attn_const_shift_kernel(qT_ref, k_ref, vT_ref, gq_ref, tab_ref, shift_ref, oT_ref, qn_ref, p_ref, *, nope):
    hb, nt = qT_ref.shape[1], tab_ref.shape[0]
    heads_per_kv = hb // k_ref.shape[1]
    n_items = hb * nt
    shift = shift_ref[...]

    def prep(n):
        _prep_q(qT_ref, gq_ref, tab_ref, qn_ref, n, nope)

    def probs(n):
        s = jnp.dot(k_ref[0, (n // nt) // heads_per_kv], qn_ref[...], preferred_element_type=F32)
        p_ref[...] = jnp.exp2(s - shift).astype(BF16)

    def output(n):
        _store_out(oT_ref, vT_ref, p_ref[...], n, heads_per_kv)

    prep(0)
    probs(0)
    prep(1)

    def body(n, carry):
        output(n - 1)
        probs(n)
        prep(jnp.minimum(n + 1, n_items - 1))
        return carry

    lax.fori_loop(1, n_items, body, 0)
    output(n_items - 1)


def _out_ffn_kernel(x_ref, oT_ref, wo_ref, gf_ref, wg_ref, wu_ref, wd_ref, out_ref, act_ref):
    nc, _, tq = oT_ref.shape[1:]
    cols = min(tq, Q_STORE_TILE)
    y = [lax.dot_general(oT_ref[0, c, :, j * cols:(j + 1) * cols], wo_ref[...], _TN, preferred_element_type=F32)
         for c in range(nc) for j in range(tq // cols)]
    x1 = x_ref[0] + (y[0] if len(y) == 1 else jnp.concatenate(y, axis=0))
    h = _rms_tok(x1, gf_ref[...]).astype(BF16)
    for c in range(act_ref.shape[1] // FF_CHUNK):
        sl = slice(c * FF_CHUNK, (c + 1) * FF_CHUNK)
        g = jnp.dot(h, wg_ref[:, sl], preferred_element_type=F32)
        u = jnp.dot(h, wu_ref[:, sl], preferred_element_type=F32)
        act_ref[:, sl] = (g / (1.0 + jnp.exp(-g)) * u).astype(BF16)
    out_ref[0] = x1 + jnp.dot(act_ref[...], wd_ref[...], preferred_element_type=F32)


def _resident(shape):
    return pl.BlockSpec(shape, lambda *_: (0,) * len(shape), pipeline_mode=pl.Buffered(1))


def _params(vmem_mib, ndims):
    return pltpu.CompilerParams(dimension_semantics=("parallel",) * ndims, vmem_limit_bytes=vmem_mib * MIB)


def _rope_tables(s, rot_dim, lane_off):
    half = rot_dim // 2
    nf = half // 2
    tok = jnp.arange(s, dtype=jnp.int32)
    row = (tok // GRID_W).astype(F32)
    col = (tok % GRID_W).astype(F32)
    inv = ROPE_THETA ** (-jnp.arange(0, half, 2, dtype=F32) / half)
    ar = row[:, None] * inv[None, :]
    ac = col[:, None] * inv[None, :]
    cr, sr, cc, sc = jnp.cos(ar), jnp.sin(ar), jnp.cos(ac), jnp.sin(ac)
    tabq = jnp.stack([cr.T, sr.T, cc.T, sc.T]).reshape(4, nf, s // Q_TILE, Q_TILE).transpose(2, 0, 1, 3)
    lo = jnp.zeros((s, lane_off), F32)
    hi = jnp.zeros((s, LANES - lane_off - rot_dim), F32)
    cos_k = jnp.concatenate([lo, cr, cr, cc, cc, hi], axis=1)
    sin_k = jnp.concatenate([lo, -sr, sr, -sc, sc, hi], axis=1)
    return tabq, cos_k, sin_k


def _pad_heads_cols(w, heads, d):
    w = w.reshape(w.shape[0], heads, d)
    return jnp.pad(w, ((0, 0), (0, 0), (0, LANES - d))).reshape(w.shape[0], heads * LANES)


def _proj_out(b, s, t, heads, kv_heads, dq, dv):
    specs = [
        pl.BlockSpec((1, heads, t // Q_STORE_TILE, dq, Q_STORE_TILE), lambda bi, i: (bi, 0, i, 0, 0)),
        pl.BlockSpec((1, kv_heads, t, LANES), lambda bi, i: (bi, 0, i, 0)),
        pl.BlockSpec((1, kv_heads, dv + BF16_ROWS, t), lambda bi, i: (bi, 0, 0, i)),
    ]
    shapes = [
        jax.ShapeDtypeStruct((b, heads, s // Q_STORE_TILE, dq, Q_STORE_TILE), BF16),
        jax.ShapeDtypeStruct((b, kv_heads, s, LANES), BF16),
        jax.ShapeDtypeStruct((b, kv_heads, dv + BF16_ROWS, s), BF16),
    ]
    return specs, shapes


def _gqa_proj(x, norm_g, w_qkv, k_norm, tabs):
    b, s, d = x.shape
    t = TOKEN_TILE
    nq = GQA_HEADS * GQA_HD
    nk = GQA_KV_HEADS * GQA_HD
    _, cos_k, sin_k = tabs
    wqT = w_qkv[:, :nq].T.astype(BF16)
    wk = _pad_heads_cols(w_qkv[:, nq:nq + nk], GQA_KV_HEADS, GQA_HD).astype(BF16)
    wvT = w_qkv[:, nq + nk:].T.astype(BF16)
    gk = jnp.pad(k_norm, (0, LANES - GQA_HD)).reshape(1, LANES)
    out_specs, out_shape = _proj_out(b, s, t, GQA_HEADS, GQA_KV_HEADS, GQA_HD, GQA_HD)
    return pl.pallas_call(
        _gqa_proj_kernel,
        grid=(b, s // t),
        in_specs=[
            pl.BlockSpec((1, t, d), lambda bi, i: (bi, i, 0)),
            _resident((1, d)),
            _resident((nq, d)),
            _resident((d, GQA_KV_HEADS * LANES)),
            _resident((nk, d)),
            _resident((1, LANES)),
            pl.BlockSpec((t, LANES), lambda bi, i: (i, 0)),
            pl.BlockSpec((t, LANES), lambda bi, i: (i, 0)),
        ],
        out_specs=out_specs,
        out_shape=out_shape,
        compiler_params=_params(40, 2),
        name="gqa_proj",
    )(x, norm_g.reshape(1, d), wqT, wk, wvT, gk, cos_k, sin_k)


def _mla_proj(x, norm_g, w_in, q_lora_norm, w_uq, kv_lora_norm, w_ukv, k_norm, tabs):
    b, s, d = x.shape
    t = TOKEN_TILE
    _, cos_k, sin_k = tabs
    nlat = Q_LORA + KV_LORA
    winT = w_in[:, :nlat].T.astype(BF16)
    w_kr = jnp.pad(w_in[:, nlat:], ((0, 0), (MLA_NOPE, LANES - MLA_QK)))
    wink = jnp.concatenate([w_in[:, Q_LORA:nlat], w_kr], axis=1).astype(BF16)
    wuqT = w_uq.T.astype(BF16)
    wukv = w_ukv.reshape(KV_LORA, MLA_HEADS, MLA_NOPE + MLA_V)
    wuk = _pad_heads_cols(wukv[:, :, :MLA_NOPE].reshape(KV_LORA, MLA_HEADS * MLA_NOPE), MLA_HEADS, MLA_NOPE).astype(BF16)
    wuvT = wukv[:, :, MLA_NOPE:].reshape(KV_LORA, MLA_HEADS * MLA_V).T.astype(BF16)
    gkn = jnp.pad(k_norm[:MLA_NOPE], (0, LANES - MLA_NOPE)).reshape(1, LANES)
    gkr = jnp.pad(k_norm[MLA_NOPE:], (MLA_NOPE, LANES - MLA_QK)).reshape(1, LANES)
    out_specs, out_shape = _proj_out(b, s, t, MLA_HEADS, MLA_HEADS, MLA_QK, MLA_V)
    return pl.pallas_call(
        _mla_proj_kernel,
        grid=(b, s // t),
        in_specs=[
            pl.BlockSpec((1, t, d), lambda bi, i: (bi, i, 0)),
            _resident((1, d)),
            _resident((nlat, d)),
            _resident((d, KV_LORA + LANES)),
            _resident((Q_LORA, 1)),
            _resident((KV_LORA, 1)),
            _resident((1, KV_LORA)),
            _resident((MLA_HEADS * MLA_QK, Q_LORA)),
            _resident((MLA_HEADS * MLA_V, KV_LORA)),
            _resident((KV_LORA, MLA_HEADS * LANES)),
            _resident((1, LANES)),
            _resident((1, LANES)),
            pl.BlockSpec((t, LANES), lambda bi, i: (i, 0)),
            pl.BlockSpec((t, LANES), lambda bi, i: (i, 0)),
        ],
        out_specs=out_specs,
        out_shape=out_shape,
        compiler_params=_params(48, 2),
        name="mla_proj",
    )(x, norm_g.reshape(1, d), winT, wink, q_lora_norm.reshape(Q_LORA, 1), kv_lora_norm.reshape(KV_LORA, 1),
      kv_lora_norm.reshape(1, KV_LORA), wuqT, wuvT, wuk, gkn, gkr, cos_k, sin_k)


def _attn_head_block(heads, kv_heads, nt, dq, tq, s, dva, scratch_bytes):
    hb = heads
    while hb > heads // kv_heads:
        nk = hb * kv_heads // heads
        block_bytes = 2 * (hb * nt * dq * tq + nk * s * LANES + nk * dva * s + nt * hb * (dva - BF16_ROWS) * tq)
        if 2 * block_bytes + scratch_bytes <= ATTN_VMEM_BUDGET:
            break
        hb //= 2
    return hb


def _score_bound(q_gain, k_gain, nope):
    def norm_bound(g):
        parts = [g[:nope], g[nope:]] if nope else [g]
        return jnp.sqrt(sum(p.shape[0] * jnp.max(jnp.abs(p)) ** 2 for p in parts))
    return NORM_SLACK * NORM_SLACK * norm_bound(q_gain) * norm_bound(k_gain)


def _attention(qT, k, vT, q_gain, k_gain, tabq, nope):
    b, heads, nts, dq, tqs = qT.shape
    nt, _, nf, tq = tabq.shape
    kv_heads, s = k.shape[1], k.shape[2]
    dva = vT.shape[2]
    dv = dva - BF16_ROWS
    def call(kernel, name, with_scores, extra_specs, *extra_args):
        scratch = [pltpu.VMEM((LANES, tq), BF16)]
        scratch += [pltpu.VMEM((s, tq), F32)] if with_scores else []
        scratch += [pltpu.VMEM((s, tq), BF16)]
        hb = _attn_head_block(heads, kv_heads, nt, dq, tq, s, dva, s * tq * (6 if with_scores else 2))
        nk = hb * kv_heads // heads
        return pl.pallas_call(
            functools.partial(kernel, nope=nope),
            grid=(b, heads // hb),
            in_specs=[
                pl.BlockSpec((1, hb, nts, dq, tqs), lambda bi, g: (bi, g, 0, 0, 0)),
                pl.BlockSpec((1, nk, s, LANES), lambda bi, g: (bi, g, 0, 0)),
                pl.BlockSpec((1, nk, dva, s), lambda bi, g: (bi, g, 0, 0)),
                _resident((dq, 1)),
                _resident((nt, 4, nf, tq)),
            ] + extra_specs,
            out_specs=pl.BlockSpec((1, nt, hb * dv, tq), lambda bi, g: (bi, 0, g, 0)),
            out_shape=jax.ShapeDtypeStruct((b, nt, heads * dv, tq), BF16),
            scratch_shapes=scratch,
            compiler_params=_params(48, 2),
            name=name,
        )(qT, k, vT, q_gain.reshape(dq, 1), tabq, *extra_args)

    shift = _score_bound(q_gain, k_gain, nope)

    def const_shift(shift):
        return call(_attn_const_shift_kernel, "attention_const_shift", False, [_resident((1, 1))], shift.reshape(1, 1))

    def exact_max(shift):
        del shift
        return call(_attn_kernel, "attention_exact_max", True, [])

    return lax.cond(shift <= MAX_CONST_SHIFT, const_shift, exact_max, shift)


def _out_ffn(x, oT, w_o, ffn_g, w_gate_up, w_down):
    b, s, d = x.shape
    t = FFN_TOKEN_TILE
    dff = w_down.shape[0]
    tq = oT.shape[3]
    wg = w_gate_up[:, :dff].astype(BF16)
    wu = w_gate_up[:, dff:].astype(BF16)
    return pl.pallas_call(
        _out_ffn_kernel,
        grid=(b, s // t),
        in_specs=[
            pl.BlockSpec((1, t, d), lambda bi, i: (bi, i, 0)),
            (pl.BlockSpec((1, t // tq, oT.shape[2], tq), lambda bi, i: (bi, i, 0, 0)) if tq <= t else
             pl.BlockSpec((1, 1, oT.shape[2], t), lambda bi, i: (bi, i * t // tq, 0, i % (tq // t)))),
            _resident(w_o.shape),
            _resident((1, d)),
            _resident((d, dff)),
            _resident((d, dff)),
            _resident((dff, d)),
        ],
        out_specs=pl.BlockSpec((1, t, d), lambda bi, i: (bi, i, 0)),
        out_shape=jax.ShapeDtypeStruct((b, s, d), F32),
        scratch_shapes=[pltpu.VMEM((t, dff), BF16)],
        compiler_params=_params(60, 2),
        name="out_ffn",
    )(x, oT, w_o.astype(BF16), ffn_g.reshape(1, d), wg, wu, w_down.astype(BF16))


def kernel(x, mla_norm, mla_w_in, mla_q_lora_norm, mla_w_uq, mla_kv_lora_norm, mla_w_ukv, mla_q_norm, mla_k_norm, mla_w_o, gqa_norm, gqa_w_qkv, gqa_q_norm, gqa_k_norm, gqa_w_o, ffn_norm, ffn_w_gate_up, ffn_w_down):
    s = x.shape[1]
    mla_tabs = _rope_tables(s, MLA_ROPE, MLA_NOPE)
    gqa_tabs = _rope_tables(s, GQA_HD, 0)
    depth = ffn_norm.shape[0]
    for i in range(depth):
        j = i // 2
        if i % 2 == 0:
            qT, k, vT = _mla_proj(x, mla_norm[j], mla_w_in[j], mla_q_lora_norm[j], mla_w_uq[j],
                                  mla_kv_lora_norm[j], mla_w_ukv[j], mla_k_norm[j], mla_tabs)
            q_gain = mla_q_norm[j] * (MLA_QK ** -0.5 * LOG2E)
            oT = _attention(qT, k, vT, q_gain, mla_k_norm[j], mla_tabs[0], MLA_NOPE)
            w_o = mla_w_o[j]
        else:
            qT, k, vT = _gqa_proj(x, gqa_norm[j], gqa_w_qkv[j], gqa_k_norm[j], gqa_tabs)
            q_gain = gqa_q_norm[j] * (GQA_HD ** -0.5 * LOG2E)
            oT = _attention(qT, k, vT, q_gain, gqa_k_norm[j], gqa_tabs[0], 0)
            w_o = gqa_w_o[j]
        x = _out_ffn(x, oT, w_o, ffn_norm[i], ffn_w_gate_up[i], ffn_w_down[i])
    return x
```
